```python
import jax, jax.numpy as jnp
from jax import lax
import numpy as np

D_MODEL = 4096
BATCH = 4
SEQ = 4096
DEPTH = 1
DEC_BATCH = 32
DEC_SEQ = 16
PAST_LEN = 4096

CHUNK = 64
MIX_WIDTH = D_MODEL
WIDTH_A = MIX_WIDTH // 2
WIDTH_B = MIX_WIDTH - WIDTH_A
GMLP_CHUNK = 128
HEADS_A = 8
HEAD_DIM_A = WIDTH_A // HEADS_A
HGRN_EXPAND = 128
HEADS_B = WIDTH_B // HGRN_EXPAND
DK_B = HGRN_EXPAND
DV_B = WIDTH_B // HEADS_B
D_FF = 4 * D_MODEL
N_IN = 2 * WIDTH_A + 4 * WIDTH_B
SPLITS = [WIDTH_A, 2 * WIDTH_A, 2 * WIDTH_A + WIDTH_B, 2 * WIDTH_A + 2 * WIDTH_B, 2 * WIDTH_A + 3 * WIDTH_B]
EPS = 1e-6

kernel_name = 'hybrid_gmlp_hgrn2_streaming_step'


def rmsnorm(x, g):
    x32 = x.astype(jnp.float32)
    y = x32 * lax.rsqrt(jnp.mean(x32 * x32, axis=-1, keepdims=True) + EPS)
    return (y * g.astype(jnp.float32)).astype(x.dtype)


def chunk_causal_mask(n):
    blk = np.arange(n) // CHUNK
    return jnp.asarray(blk[None, :] <= blk[:, None])


def gmlp_spatial(u, v, w_s, b_s):
    B, T, H, Dh = v.shape
    L = min(T, GMLP_CHUNK)
    n = T // L
    w = jnp.where(chunk_causal_mask(L)[None], w_s[:, :L, :L], 0.0).astype(v.dtype)
    vc = v.reshape(B, n, L, H, Dh)
    mixed = jnp.einsum('hij,bnjhd->bnihd', w, vc) + b_s[:, :L].T.astype(v.dtype)[None, None, :, :, None]
    return u * mixed.reshape(B, T, H, Dh)


def hgrn2_chunkwise(q, k, v, log_f, s0):
    f32 = jnp.float32
    B, T, H, DK = q.shape
    DV = v.shape[-1]
    L = min(T, CHUNK)
    n = T // L
    qc = q.astype(f32).reshape(B, n, L, H, DK)
    kc = k.astype(f32).reshape(B, n, L, H, DK)
    vc = v.astype(f32).reshape(B, n, L, H, DV)
    b = jnp.cumsum(log_f.astype(f32).reshape(B, n, L, H, DK), axis=2)
    b_ref = b[:, :, L // 2:L // 2 + 1]
    b_end = b[:, :, -1]
    q_rel = qc * jnp.exp(b - b_ref)
    k_rel = kc * jnp.exp(b_ref - b)
    scores = jnp.einsum('bnthd,bnshd->bnhts', q_rel, k_rel)
    scores = jnp.where(jnp.tril(jnp.ones((L, L), bool)), scores, 0.0)
    o_intra = jnp.einsum('bnhts,bnshv->bnthv', scores, vc)
    k_end = kc * jnp.exp(b_end[:, :, None] - b)
    ds = jnp.einsum('bnshd,bnshv->bnhdv', k_end, vc)
    decay = jnp.exp(b_end)

    def step(s, inp):
        dec, d = inp
        return dec[..., None] * s + d, s

    s_final, s_start = lax.scan(step, s0.astype(f32), (jnp.moveaxis(decay, 1, 0), jnp.moveaxis(ds, 1, 0)))
    s_start = jnp.moveaxis(s_start, 0, 1)
    o_inter = jnp.einsum('bnthd,bnhdv->bnthv', qc * jnp.exp(b), s_start)
    return (o_intra + o_inter).reshape(B, T, H, DV), s_final


def trunk_layer(x, s0, lb, norm1, w_in, w_s, b_s, norm_v, norm_o, w_out, norm2, w_up, w_down):
    B, T, _ = x.shape
    h = rmsnorm(x, norm1)
    proj = h @ w_in.astype(h.dtype)
    u, v, q, fz, i, g = jnp.split(proj, SPLITS, axis=-1)
    u = jax.nn.gelu(u).reshape(B, T, HEADS_A, HEAD_DIM_A)
    v = rmsnorm(jax.nn.gelu(v).reshape(B, T, HEADS_A, HEAD_DIM_A), norm_v)
    y_a = gmlp_spatial(u, v, w_s, b_s).reshape(B, T, WIDTH_A)
    f = lb + (1.0 - lb) * jax.nn.sigmoid(fz.astype(jnp.float32))
    k = 1.0 - f
    log_f = jnp.log(f)
    qh = jax.nn.silu(q).reshape(B, T, HEADS_B, DK_B)
    o, s_new = hgrn2_chunkwise(qh, k.reshape(B, T, HEADS_B, DK_B), i.reshape(B, T, HEADS_B, DV_B),
                               log_f.reshape(B, T, HEADS_B, DK_B), s0)
    o = rmsnorm(o, norm_o) * jax.nn.silu(g.reshape(B, T, HEADS_B, DV_B)).astype(jnp.float32)
    y_b = o.reshape(B, T, WIDTH_B).astype(x.dtype)
    x = x + jnp.concatenate([y_a, y_b], axis=-1) @ w_out.astype(x.dtype)
    hid = jnp.square(jax.nn.relu(rmsnorm(x, norm2) @ w_up.astype(x.dtype)))
    x = x + hid @ w_down.astype(x.dtype)
    return x, s_new, v.reshape(B, T, WIDTH_A)


def setup_inputs(seed: int = 0) -> dict:
    key = jax.random.key(seed)
    ks = jax.random.split(key, 16)
    f32 = jnp.float32

    def nrm(k, shape, scale):
        return jax.random.normal(k, shape, f32) * scale

    return {
        'x_prompt': nrm(ks[0], (BATCH, SEQ, D_MODEL), 1.0),
        'x_sample': nrm(ks[1], (DEC_BATCH, DEC_SEQ, D_MODEL), 1.0),
        'state_hgrn': nrm(ks[2], (DEPTH, DEC_BATCH, HEADS_B, DK_B, DV_B), 0.5),
        'norm1': 1.0 + nrm(ks[3], (DEPTH, D_MODEL), 0.05),
        'w_in': nrm(ks[4], (DEPTH, D_MODEL, N_IN), D_MODEL ** -0.5),
        'w_s': nrm(ks[5], (DEPTH, HEADS_A, GMLP_CHUNK, GMLP_CHUNK), GMLP_CHUNK ** -0.5),
        'b_s': 1.0 + nrm(ks[6], (DEPTH, HEADS_A, GMLP_CHUNK), 0.1),
        'norm_v': 1.0 + nrm(ks[7], (DEPTH, HEADS_A, HEAD_DIM_A), 0.05),
        'lb_logits': nrm(ks[8], (DEPTH + 1, WIDTH_B), 0.1),
        'norm_o': 1.0 + nrm(ks[9], (DEPTH, HEADS_B, DV_B), 0.05),
        'w_out': nrm(ks[10], (DEPTH, MIX_WIDTH, D_MODEL), MIX_WIDTH ** -0.5),
        'norm2': 1.0 + nrm(ks[11], (DEPTH, D_MODEL), 0.05),
        'w_up': nrm(ks[12], (DEPTH, D_MODEL, D_FF), D_MODEL ** -0.5),
        'w_down': nrm(ks[13], (DEPTH, D_FF, D_MODEL), D_FF ** -0.5),
        'norm_f': 1.0 + nrm(ks[14], (D_MODEL,), 0.05),
    }


def reference(x_prompt, x_sample, state_hgrn, norm1, w_in, w_s, b_s, norm_v, lb_logits, norm_o,
              w_out, norm2, w_up, w_down, norm_f):
    lower_bounds = jnp.cumsum(jax.nn.softmax(lb_logits.astype(jnp.float32), axis=0), axis=0)
    s0_prompt = jnp.zeros((x_prompt.shape[0], HEADS_B, DK_B, DV_B), jnp.float32)
    xp, xs = x_prompt, x_sample
    s_prompt, s_sample, v_sample = [], [], []
    for l in range(DEPTH):
        params = (norm1[l], w_in[l], w_s[l], b_s[l], norm_v[l], norm_o[l], w_out[l], norm2[l], w_up[l], w_down[l])
        xp, sp, _ = trunk_layer(xp, s0_prompt, lower_bounds[l], *params)
        xs, ss, vs = trunk_layer(xs, state_hgrn[l], lower_bounds[l], *params)
        s_prompt.append(sp)
        s_sample.append(ss)
        v_sample.append(vs)
    y_prompt = rmsnorm(xp, norm_f)
    y_sample = rmsnorm(xs, norm_f)
    return (y_prompt, y_sample, jnp.stack(s_prompt), jnp.stack(s_sample), jnp.stack(v_sample))
```

```python
import functools

import jax
import jax.numpy as jnp
from jax import lax
from jax.experimental import pallas as pl
from jax.experimental.pallas import tpu as pltpu

EPS = 1e-6
CHUNK = 64
GMLP_CHUNK = 128
HEADS_A = 8
DK_B = 128
BF16 = jnp.bfloat16
F32 = jnp.float32

V7X_VMEM_BYTES = 64 * 1024 * 1024
VMEM_LIMIT_BYTES = V7X_VMEM_BYTES - 8 * 1024 * 1024


def _params(*semantics):
    return pltpu.CompilerParams(dimension_semantics=semantics, vmem_limit_bytes=VMEM_LIMIT_BYTES)


def _rms_scale(x):
    return lax.rsqrt(jnp.mean(x * x, axis=-1, keepdims=True) + EPS)


def _rmsnorm_kernel(x_ref, g_ref, o_ref):
    x = x_ref[...]
    o_ref[...] = (x * _rms_scale(x) * g_ref[...]).astype(o_ref.dtype)


def rmsnorm_bf16(x, gain, *, tm):
    m, d = x.shape
    return pl.pallas_call(
        _rmsnorm_kernel,
        grid=(m // tm,),
        in_specs=[pl.BlockSpec((tm, d), lambda i: (i, 0)),
                  pl.BlockSpec((1, d), lambda i: (0, 0))],
        out_specs=pl.BlockSpec((tm, d), lambda i: (i, 0)),
        out_shape=jax.ShapeDtypeStruct((m, d), BF16),
        compiler_params=_params("parallel"),
        name="rmsnorm_bf16",
    )(x, gain.reshape(1, d))


def _inproj_kernel(h_ref, w_ref, nv_ref, lb_ref, o_ref, *, tiles_per_group, head_dim_a, layer):
    j = pl.program_id(1)
    group = j // tiles_per_group
    acc = jnp.dot(h_ref[...], w_ref[...], preferred_element_type=F32)

    @pl.when(group == 0)
    def _():
        o_ref[...] = jax.nn.gelu(acc)

    @pl.when(group == 1)
    def _():
        a = jax.nn.gelu(acc)
        for s in range(acc.shape[1] // head_dim_a):
            sl = slice(s * head_dim_a, (s + 1) * head_dim_a)
            blk = a[:, sl]
            o_ref[:, sl] = blk * _rms_scale(blk) * nv_ref[:, sl]

    @pl.when(jnp.logical_or(group == 2, group == 5))
    def _():
        o_ref[...] = acc * jax.nn.sigmoid(acc)

    @pl.when(group == 3)
    def _():
        lg = lb_ref[...]
        e = jnp.exp(lg - jnp.max(lg, axis=0, keepdims=True))
        sm = e / jnp.sum(e, axis=0, keepdims=True)
        lb = jnp.sum(sm[:layer + 1, :], axis=0, keepdims=True)
        o_ref[...] = lb + (1.0 - lb) * jax.nn.sigmoid(acc)

    @pl.when(group == 4)
    def _():
        o_ref[...] = acc


def inproj(h, w_in, norm_v_flat, lb_logits, *, layer, tm, tn):
    m, d = h.shape
    n = w_in.shape[1]
    width = n // 6
    tpg = width // tn
    clip = lambda j, g: jnp.clip(j - g * tpg, 0, tpg - 1)
    kern = functools.partial(_inproj_kernel, tiles_per_group=tpg,
                             head_dim_a=width // HEADS_A, layer=layer)
    return pl.pallas_call(
        kern,
        grid=(m // tm, n // tn),
        in_specs=[pl.BlockSpec((tm, d), lambda i, j: (i, 0)),
                  pl.BlockSpec((d, tn), lambda i, j: (0, j)),
                  pl.BlockSpec((1, tn), lambda i, j: (0, clip(j, 1))),
                  pl.BlockSpec((lb_logits.shape[0], tn), lambda i, j: (0, clip(j, 3)))],
        out_specs=pl.BlockSpec((tm, tn), lambda i, j: (i, j)),
        out_shape=jax.ShapeDtypeStruct((m, n), F32),
        compiler_params=_params("parallel", "arbitrary"),
        name="inproj",
    )(h, w_in, norm_v_flat, lb_logits)


def _gmlp_kernel(u_ref, v_ref, w_ref, b_ref, o_ref, *, span):
    row = lax.broadcasted_iota(jnp.int32, (span, span), 0)
    col = lax.broadcasted_iota(jnp.int32, (span, span), 1)
    w = jnp.where(col // CHUNK <= row // CHUNK, w_ref[:span, :span], 0.0).astype(BF16)
    bias = b_ref[:span, :]
    for c in range(u_ref.shape[0] // span):
        sl = slice(c * span, (c + 1) * span)
        mixed = jnp.dot(w, v_ref[sl, :].astype(BF16), preferred_element_type=F32) + bias
        o_ref[sl, :] = (u_ref[sl, :] * mixed).astype(o_ref.dtype)


def gmlp(act, w_s, b_s, *, span, tb):
    m = act.shape[0]
    width = act.shape[1] // 6
    hd = width // HEADS_A
    kern = functools.partial(_gmlp_kernel, span=span)
    return pl.pallas_call(
        kern,
        grid=(m // tb, HEADS_A),
        in_specs=[pl.BlockSpec((tb, hd), lambda t, h: (t, h)),
                  pl.BlockSpec((tb, hd), lambda t, h: (t, HEADS_A + h)),
                  pl.BlockSpec((None, GMLP_CHUNK, GMLP_CHUNK), lambda t, h: (h, 0, 0)),
                  pl.BlockSpec((None, GMLP_CHUNK, 1), lambda t, h: (h, 0, 0))],
        out_specs=pl.BlockSpec((tb, hd), lambda t, h: (t, h)),
        out_shape=jax.ShapeDtypeStruct((m, width), BF16),
        compiler_params=_params("parallel", "parallel"),
        name="gmlp",
    )(act, act, w_s, b_s.reshape(HEADS_A, GMLP_CHUNK, 1))


def _split3_bf16(x):
    hi = x.astype(BF16)
    r = x - hi.astype(F32)
    mid = r.astype(BF16)
    lo = (r - mid.astype(F32)).astype(BF16)
    return hi, mid, lo


def _hgrn_kernel(q_ref, f_ref, i_ref, g_ref, no_ref, s0_ref, y_ref, s_ref, st_ref, *, blk, has_s0):
    tb = pl.program_id(2)
    n_streams, tokens, _ = q_ref.shape

    @pl.when(tb == 0)
    def _():
        if has_s0:
            for s in range(n_streams):
                st_ref[s] = s0_ref[s].T
        else:
            st_ref[...] = jnp.zeros_like(st_ref)

    row = lax.broadcasted_iota(jnp.int32, (blk, blk), 0)
    col = lax.broadcasted_iota(jnp.int32, (blk, blk), 1)
    causal = col <= row
    tri = causal.astype(BF16)
    gain = no_ref[...]
    nt = (((1,), (1,)), ((), ()))
    tn = (((0,), (0,)), ((), ()))

    for s in range(n_streams):
        for c in range(tokens // blk):
            sl = slice(c * blk, (c + 1) * blk)
            f = f_ref[s, sl, :]
            k = 1.0 - f
            q = q_ref[s, sl, :]
            v = i_ref[s, sl, :].astype(BF16)
            hi, mid, lo = _split3_bf16(jnp.log(f))
            b = (jnp.dot(tri, hi, preferred_element_type=F32)
                 + jnp.dot(tri, mid, preferred_element_type=F32)
                 + jnp.dot(tri, lo, preferred_element_type=F32))
            b_ref = b[blk // 2:blk // 2 + 1, :]
            b_end = b[blk - 1:blk, :]
            q_rel = (q * jnp.exp(b - b_ref)).astype(BF16)
            k_rel = (k * jnp.exp(b_ref - b)).astype(BF16)
            scores = lax.dot_general(q_rel, k_rel, nt, preferred_element_type=F32)
            scores = jnp.where(causal, scores, 0.0).astype(BF16)
            o = jnp.dot(scores, v, preferred_element_type=F32)
            st = st_ref[s]
            o = o + lax.dot_general((q * jnp.exp(b)).astype(BF16), st.astype(BF16), nt,
                                    preferred_element_type=F32)
            k_end = (k * jnp.exp(b_end - b)).astype(BF16)
            st_ref[s] = st * jnp.exp(b_end) + lax.dot_general(v, k_end, tn, preferred_element_type=F32)
            y_ref[s, sl, :] = (o * _rms_scale(o) * gain * g_ref[s, sl, :]).astype(y_ref.dtype)

    @pl.when(tb == pl.num_programs(2) - 1)
    def _():
        for s in range(n_streams):
            s_ref[s] = st_ref[s].T


def hgrn(act3, norm_o, s0, *, blk, sb, tb):
    ns, t, n = act3.shape
    width = n // 6
    heads = width // DK_B
    has_s0 = s0 is not None
    col = lambda grp: (lambda b, h, c: (b, c, grp * heads + h))
    state_spec = pl.BlockSpec((sb, None, DK_B, DK_B), lambda b, h, c: (b, h, 0, 0))
    if not has_s0:
        s0 = jnp.zeros((ns, heads, DK_B, DK_B), F32)
    kern = functools.partial(_hgrn_kernel, blk=blk, has_s0=has_s0)
    return pl.pallas_call(
        kern,
        grid=(ns // sb, heads, t // tb),
        in_specs=[pl.BlockSpec((sb, tb, DK_B), col(2)),
                  pl.BlockSpec((sb, tb, DK_B), col(3)),
                  pl.BlockSpec((sb, tb, DK_B), col(4)),
                  pl.BlockSpec((sb, tb, DK_B), col(5)),
                  pl.BlockSpec((None, 1, DK_B), lambda b, h, c: (h, 0, 0)),
                  state_spec],
        out_specs=[pl.BlockSpec((sb, tb, DK_B), lambda b, h, c: (b, c, h)), state_spec],
        out_shape=[jax.ShapeDtypeStruct((ns, t, width), BF16),
                   jax.ShapeDtypeStruct((ns, heads, DK_B, DK_B), F32)],
        scratch_shapes=[pltpu.VMEM((sb, DK_B, DK_B), F32)],
        compiler_params=_params("parallel", "parallel", "arbitrary"),
        name="hgrn",
    )(act3, act3, act3, act3, norm_o.reshape(heads, 1, DK_B), s0)


def _outproj_kernel(x_ref, ya_ref, yb_ref, wa_ref, wb_ref, o_ref):
    o_ref[...] = (x_ref[...]
                  + jnp.dot(ya_ref[...], wa_ref[...], preferred_element_type=F32)
                  + jnp.dot(yb_ref[...], wb_ref[...], preferred_element_type=F32))


def outproj(x, ya, yb, w_out, *, tm, tn):
    m, d = x.shape
    ka = ya.shape[1]
    return pl.pallas_call(
        _outproj_kernel,
        grid=(m // tm, d // tn),
        in_specs=[pl.BlockSpec((tm, tn), lambda i, j: (i, j)),
                  pl.BlockSpec((tm, ka), lambda i, j: (i, 0)),
                  pl.BlockSpec((tm, ka), lambda i, j: (i, 0)),
                  pl.BlockSpec((ka, tn), lambda i, j: (0, j)),
                  pl.BlockSpec((ka, tn), lambda i, j: (1, j))],
        out_specs=pl.BlockSpec((tm, tn), lambda i, j: (i, j)),
        out_shape=jax.ShapeDtypeStruct((m, d), F32),
        compiler_params=_params("parallel", "arbitrary"),
        name="outproj",
    )(x, ya, yb, w_out, w_out)


def _mlp_kernel(x_ref, n2_ref, wu_ref, wd_ref, nf_ref, o_ref, h_ref, *, final_norm):
    f = pl.program_id(1)

    @pl.when(f == 0)
    def _():
        x = x_ref[...]
        h_ref[...] = (x * _rms_scale(x) * n2_ref[...]).astype(h_ref.dtype)
        o_ref[...] = x

    hid = jnp.dot(h_ref[...], wu_ref[...], preferred_element_type=F32)
    hid = jnp.square(jnp.maximum(hid, 0.0)).astype(BF16)
    o_ref[...] += jnp.dot(hid, wd_ref[...], preferred_element_type=F32)

    if final_norm:
        @pl.when(f == pl.num_programs(1) - 1)
        def _():
            x = o_ref[...]
            o_ref[...] = x * _rms_scale(x) * nf_ref[...]


def mlp(x, norm2, w_up, w_down, norm_f, *, final_norm, tm, tf):
    m, d = x.shape
    ff = w_up.shape[1]
    once = pl.Buffered(1)
    kern = functools.partial(_mlp_kernel, final_norm=final_norm)
    return pl.pallas_call(
        kern,
        grid=(m // tm, ff // tf),
        in_specs=[pl.BlockSpec((tm, d), lambda i, f: (i, 0), pipeline_mode=once),
                  pl.BlockSpec((1, d), lambda i, f: (0, 0), pipeline_mode=once),
                  pl.BlockSpec((d, tf), lambda i, f: (0, f)),
                  pl.BlockSpec((tf, d), lambda i, f: (f, 0)),
                  pl.BlockSpec((1, d), lambda i, f: (0, 0), pipeline_mode=once)],
        out_specs=pl.BlockSpec((tm, d), lambda i, f: (i, 0)),
        out_shape=jax.ShapeDtypeStruct((m, d), F32),
        scratch_shapes=[pltpu.VMEM((tm, d), BF16)],
        compiler_params=_params("parallel", "arbitrary"),
        name="mlp",
    )(x, norm2.reshape(1, d), w_up, w_down, norm_f.reshape(1, d))


def _trunk_layer(x3, s0, lb_logits, p, *, layer, final_norm, norm_f):
    ns, t, d = x3.shape
    m = ns * t
    x = x3.reshape(m, d)
    tm = min(m, 512)
    h = rmsnorm_bf16(x, p["norm1"], tm=min(m, 256))
    act = inproj(h, p["w_in"], p["norm_v"].reshape(1, -1), lb_logits, layer=layer, tm=tm, tn=512)
    width = act.shape[1] // 6
    span = min(t, GMLP_CHUNK)
    ya = gmlp(act, p["w_s"], p["b_s"], span=span, tb=min(m, 1024))
    blk = min(t, CHUNK)
    if t >= 512:
        sb, tb = 1, 512
    else:
        sb, tb = ns, t
    yb3, s_new = hgrn(act.reshape(ns, t, -1), p["norm_o"], s0, blk=blk, sb=sb, tb=tb)
    x1 = outproj(x, ya, yb3.reshape(m, width), p["w_out"], tm=tm, tn=1024)
    x2 = mlp(x1, p["norm2"], p["w_up"], p["w_down"], norm_f, final_norm=final_norm, tm=tm, tf=512)
    v_act = act[:, width:2 * width].reshape(ns, t, width)
    return x2.reshape(ns, t, d), s_new, v_act


def kernel(x_prompt, x_sample, state_hgrn, norm1, w_in, w_s, b_s, norm_v, lb_logits, norm_o,
           w_out, norm2, w_up, w_down, norm_f):
    depth = w_in.shape[0]
    lb_logits = lb_logits.astype(F32)
    xp, xs = x_prompt, x_sample
    s_prompt, s_sample, v_sample = [], [], []
    for l in range(depth):
        p = dict(norm1=norm1[l], w_in=w_in[l].astype(BF16), w_s=w_s[l], b_s=b_s[l], norm_v=norm_v[l],
                 norm_o=norm_o[l], w_out=w_out[l].astype(BF16), norm2=norm2[l],
                 w_up=w_up[l].astype(BF16), w_down=w_down[l].astype(BF16))
        last = l == depth - 1
        xp, sp, _ = _trunk_layer(xp, None, lb_logits, p, layer=l, final_norm=last, norm_f=norm_f)
        xs, ss, vs = _trunk_layer(xs, state_hgrn[l], lb_logits, p, layer=l, final_norm=last, norm_f=norm_f)
        s_prompt.append(sp)
        s_sample.append(ss)
        v_sample.append(vs)
    return (xp, xs, jnp.stack(s_prompt), jnp.stack(s_sample), jnp.stack(v_sample))
```

```python
import functools

import jax
import jax.numpy as jnp
from jax import lax
from jax.experimental import pallas as pl
from jax.experimental.pallas import tpu as pltpu

EPS = 1e-6
CHUNK = 64
GMLP_CHUNK = 128
HEADS_A = 8
DK_B = 128
BF16 = jnp.bfloat16
F32 = jnp.float32

V7X_VMEM_BYTES = 64 * 1024 * 1024
VMEM_LIMIT_BYTES = V7X_VMEM_BYTES - 8 * 1024 * 1024


def _params(*semantics):
    return pltpu.CompilerParams(dimension_semantics=semantics, vmem_limit_bytes=VMEM_LIMIT_BYTES)


def _rms_scale(x):
    return lax.rsqrt(jnp.mean(x * x, axis=-1, keepdims=True) + EPS)


def _rmsnorm_kernel(x_ref, g_ref, o_ref):
    x = x_ref[...]
    o_ref[...] = (x * _rms_scale(x) * g_ref[...]).astype(o_ref.dtype)


def rmsnorm_bf16(x, gain, *, tm):
    m, d = x.shape
    return pl.pallas_call(
        _rmsnorm_kernel,
        grid=(m // tm,),
        in_specs=[pl.BlockSpec((tm, d), lambda i: (i, 0)),
                  pl.BlockSpec((1, d), lambda i: (0, 0))],
        out_specs=pl.BlockSpec((tm, d), lambda i: (i, 0)),
        out_shape=jax.ShapeDtypeStruct((m, d), BF16),
        compiler_params=_params("parallel"),
        name="rmsnorm_bf16",
    )(x, gain.reshape(1, d))


def _inproj_kernel(h_ref, w_ref, nv_ref, lb_ref, o_ref, *, tiles_per_group, head_dim_a, layer):
    j = pl.program_id(1)
    group = j // tiles_per_group
    acc = jnp.dot(h_ref[...], w_ref[...], preferred_element_type=F32)

    @pl.when(group == 0)
    def _():
        o_ref[...] = jax.nn.gelu(acc)

    @pl.when(group == 1)
    def _():
        a = jax.nn.gelu(acc)
        for s in range(acc.shape[1] // head_dim_a):
            sl = slice(s * head_dim_a, (s + 1) * head_dim_a)
            blk = a[:, sl]
            o_ref[:, sl] = blk * _rms_scale(blk) * nv_ref[:, sl]

    @pl.when(jnp.logical_or(group == 2, group == 5))
    def _():
        o_ref[...] = acc * jax.nn.sigmoid(acc)

    @pl.when(group == 3)
    def _():
        lg = lb_ref[...]
        e = jnp.exp(lg - jnp.max(lg, axis=0, keepdims=True))
        sm = e / jnp.sum(e, axis=0, keepdims=True)
        lb = jnp.sum(sm[:layer + 1, :], axis=0, keepdims=True)
        o_ref[...] = lb + (1.0 - lb) * jax.nn.sigmoid(acc)

    @pl.when(group == 4)
    def _():
        o_ref[...] = acc


def inproj(h, w_in, norm_v_flat, lb_logits, *, layer, tm, tn):
    m, d = h.shape
    n = w_in.shape[1]
    width = n // 6
    tpg = width // tn
    clip = lambda j, g: jnp.clip(j - g * tpg, 0, tpg - 1)
    kern = functools.partial(_inproj_kernel, tiles_per_group=tpg,
                             head_dim_a=width // HEADS_A, layer=layer)
    return pl.pallas_call(
        kern,
        grid=(m // tm, n // tn),
        in_specs=[pl.BlockSpec((tm, d), lambda i, j: (i, 0)),
                  pl.BlockSpec((d, tn), lambda i, j: (0, j)),
                  pl.BlockSpec((1, tn), lambda i, j: (0, clip(j, 1))),
                  pl.BlockSpec((lb_logits.shape[0], tn), lambda i, j: (0, clip(j, 3)))],
        out_specs=pl.BlockSpec((tm, tn), lambda i, j: (i, j)),
        out_shape=jax.ShapeDtypeStruct((m, n), F32),
        compiler_params=_params("parallel", "arbitrary"),
        name="inproj",
    )(h, w_in, norm_v_flat, lb_logits)


def _gmlp_kernel(u_ref, v_ref, w_ref, b_ref, o_ref, *, span):
    row = lax.broadcasted_iota(jnp.int32, (span, span), 0)
    col = lax.broadcasted_iota(jnp.int32, (span, span), 1)
    w = jnp.where(col // CHUNK <= row // CHUNK, w_ref[:span, :span], 0.0).astype(BF16)
    bias = b_ref[:span, :]
    for c in range(u_ref.shape[0] // span):
        sl = slice(c * span, (c + 1) * span)
        mixed = jnp.dot(w, v_ref[sl, :].astype(BF16), preferred_element_type=F32) + bias
        o_ref[sl, :] = (u_ref[sl, :] * mixed).astype(o_ref.dtype)


def gmlp(act, w_s, b_s, *, span, tb):
    m = act.shape[0]
    width = act.shape[1] // 6
    hd = width // HEADS_A
    kern = functools.partial(_gmlp_kernel, span=span)
    return pl.pallas_call(
        kern,
        grid=(m // tb, HEADS_A),
        in_specs=[pl.BlockSpec((tb, hd), lambda t, h: (t, h)),
                  pl.BlockSpec((tb, hd), lambda t, h: (t, HEADS_A + h)),
                  pl.BlockSpec((None, GMLP_CHUNK, GMLP_CHUNK), lambda t, h: (h, 0, 0)),
                  pl.BlockSpec((None, GMLP_CHUNK, 1), lambda t, h: (h, 0, 0))],
        out_specs=pl.BlockSpec((tb, hd), lambda t, h: (t, h)),
        out_shape=jax.ShapeDtypeStruct((m, width), BF16),
        compiler_params=_params("parallel", "parallel"),
        name="gmlp",
    )(act, act, w_s, b_s.reshape(HEADS_A, GMLP_CHUNK, 1))


def _split2_bf16(x):
    hi = x.astype(BF16)
    lo = (x - hi.astype(F32)).astype(BF16)
    return hi, lo


def _hgrn_kernel(q_ref, f_ref, i_ref, g_ref, no_ref, s0_ref, y_ref, s_ref, st_ref, *, blk, has_s0):
    tb = pl.program_id(2)
    n_streams, tokens, _ = q_ref.shape
    n_chunks = tokens // blk
    items = [(s, c) for s in range(n_streams) for c in range(n_chunks)]
    rows = lambda c: slice(c * blk, (c + 1) * blk)

    @pl.when(tb == 0)
    def _():
        if has_s0:
            for s in range(n_streams):
                st_ref[s] = s0_ref[s].T
        else:
            st_ref[...] = jnp.zeros_like(st_ref)

    row = lax.broadcasted_iota(jnp.int32, (blk, blk), 0)
    col = lax.broadcasted_iota(jnp.int32, (blk, blk), 1)
    causal = col <= row
    tri = causal.astype(BF16)
    gain = no_ref[...]
    nt = (((1,), (1,)), ((), ()))
    tn = (((0,), (0,)), ((), ()))

    cum = {}
    for s, c in items:
        hi, lo = _split2_bf16(jnp.log(f_ref[s, rows(c), :]))
        cum[s, c] = (jnp.dot(tri, hi, preferred_element_type=F32)
                     + jnp.dot(tri, lo, preferred_element_type=F32))

    q_rel, k_rel, q_abs, k_end, decay, val = {}, {}, {}, {}, {}, {}
    for s, c in items:
        b = cum[s, c]
        b_mid = b[blk // 2:blk // 2 + 1, :]
        b_end = b[blk - 1:blk, :]
        qr = q_ref[s, rows(c), :] * jnp.exp(b - b_mid)
        kr = (1.0 - f_ref[s, rows(c), :]) * jnp.exp(b_mid - b)
        q_rel[s, c] = qr.astype(BF16)
        k_rel[s, c] = kr.astype(BF16)
        q_abs[s, c] = (qr * jnp.exp(b_mid)).astype(BF16)
        k_end[s, c] = (kr * jnp.exp(b_end - b_mid)).astype(BF16)
        decay[s, c] = jnp.exp(b_end)
        val[s, c] = i_ref[s, rows(c), :].astype(BF16)

    scores = {}
    for it in items:
        sc = lax.dot_general(q_rel[it], k_rel[it], nt, preferred_element_type=F32)
        scores[it] = jnp.where(causal, sc, 0.0).astype(BF16)

    out, delta = {}, {}
    for it in items:
        out[it] = jnp.dot(scores[it], val[it], preferred_element_type=F32)
        delta[it] = lax.dot_general(val[it], k_end[it], tn, preferred_element_type=F32)

    for s in range(n_streams):
        st = st_ref[s]
        for c in range(n_chunks):
            out[s, c] = out[s, c] + lax.dot_general(q_abs[s, c], st.astype(BF16), nt,
                                                    preferred_element_type=F32)
            st = st * decay[s, c] + delta[s, c]
        st_ref[s] = st

    for s, c in items:
        o = out[s, c]
        y_ref[s, rows(c), :] = (o * _rms_scale(o) * gain * g_ref[s, rows(c), :]).astype(y_ref.dtype)

    @pl.when(tb == pl.num_programs(2) - 1)
    def _():
        for s in range(n_streams):
            s_ref[s] = st_ref[s].T


def hgrn(act3, norm_o, s0, *, blk, sb, tb):
    ns, t, n = act3.shape
    width = n // 6
    heads = width // DK_B
    has_s0 = s0 is not None
    col = lambda grp: (lambda b, h, c: (b, c, grp * heads + h))
    state_spec = pl.BlockSpec((sb, None, DK_B, DK_B), lambda b, h, c: (b, h, 0, 0))
    if not has_s0:
        s0 = jnp.zeros((ns, heads, DK_B, DK_B), F32)
    kern = functools.partial(_hgrn_kernel, blk=blk, has_s0=has_s0)
    return pl.pallas_call(
        kern,
        grid=(ns // sb, heads, t // tb),
        in_specs=[pl.BlockSpec((sb, tb, DK_B), col(2)),
                  pl.BlockSpec((sb, tb, DK_B), col(3)),
                  pl.BlockSpec((sb, tb, DK_B), col(4)),
                  pl.BlockSpec((sb, tb, DK_B), col(5)),
                  pl.BlockSpec((None, 1, DK_B), lambda b, h, c: (h, 0, 0)),
                  state_spec],
        out_specs=[pl.BlockSpec((sb, tb, DK_B), lambda b, h, c: (b, c, h)), state_spec],
        out_shape=[jax.ShapeDtypeStruct((ns, t, width), BF16),
                   jax.ShapeDtypeStruct((ns, heads, DK_B, DK_B), F32)],
        scratch_shapes=[pltpu.VMEM((sb, DK_B, DK_B), F32)],
        compiler_params=_params("parallel", "parallel", "arbitrary"),
        name="hgrn",
    )(act3, act3, act3, act3, norm_o.reshape(heads, 1, DK_B), s0)


def _outproj_kernel(x_ref, ya_ref, yb_ref, wa_ref, wb_ref, o_ref):
    o_ref[...] = (x_ref[...]
                  + jnp.dot(ya_ref[...], wa_ref[...], preferred_element_type=F32)
                  + jnp.dot(yb_ref[...], wb_ref[...], preferred_element_type=F32))


def outproj(x, ya, yb, w_out, *, tm, tn):
    m, d = x.shape
    ka = ya.shape[1]
    return pl.pallas_call(
        _outproj_kernel,
        grid=(m // tm, d // tn),
        in_specs=[pl.BlockSpec((tm, tn), lambda i, j: (i, j)),
                  pl.BlockSpec((tm, ka), lambda i, j: (i, 0)),
                  pl.BlockSpec((tm, ka), lambda i, j: (i, 0)),
                  pl.BlockSpec((ka, tn), lambda i, j: (0, j)),
                  pl.BlockSpec((ka, tn), lambda i, j: (1, j))],
        out_specs=pl.BlockSpec((tm, tn), lambda i, j: (i, j)),
        out_shape=jax.ShapeDtypeStruct((m, d), F32),
        compiler_params=_params("parallel", "arbitrary"),
        name="outproj",
    )(x, ya, yb, w_out, w_out)


def _mlp_kernel(x_ref, n2_ref, wu_ref, wd_ref, nf_ref, o_ref, h_ref, *, final_norm):
    f = pl.program_id(1)

    @pl.when(f == 0)
    def _():
        x = x_ref[...]
        h_ref[...] = (x * _rms_scale(x) * n2_ref[...]).astype(h_ref.dtype)
        o_ref[...] = x

    hid = jnp.dot(h_ref[...], wu_ref[...], preferred_element_type=F32)
    hid = jnp.square(jnp.maximum(hid, 0.0)).astype(BF16)
    o_ref[...] += jnp.dot(hid, wd_ref[...], preferred_element_type=F32)

    if final_norm:
        @pl.when(f == pl.num_programs(1) - 1)
        def _():
            x = o_ref[...]
            o_ref[...] = x * _rms_scale(x) * nf_ref[...]


def mlp(x, norm2, w_up, w_down, norm_f, *, final_norm, tm, tf):
    m, d = x.shape
    ff = w_up.shape[1]
    once = pl.Buffered(1)
    kern = functools.partial(_mlp_kernel, final_norm=final_norm)
    return pl.pallas_call(
        kern,
        grid=(m // tm, ff // tf),
        in_specs=[pl.BlockSpec((tm, d), lambda i, f: (i, 0), pipeline_mode=once),
                  pl.BlockSpec((1, d), lambda i, f: (0, 0), pipeline_mode=once),
                  pl.BlockSpec((d, tf), lambda i, f: (0, f)),
                  pl.BlockSpec((tf, d), lambda i, f: (f, 0)),
                  pl.BlockSpec((1, d), lambda i, f: (0, 0), pipeline_mode=once)],
        out_specs=pl.BlockSpec((tm, d), lambda i, f: (i, 0)),
        out_shape=jax.ShapeDtypeStruct((m, d), F32),
        scratch_shapes=[pltpu.VMEM((tm, d), BF16)],
        compiler_params=_params("parallel", "arbitrary"),
        name="mlp",
    )(x, norm2.reshape(1, d), w_up, w_down, norm_f.reshape(1, d))


def _trunk_layer(x3, s0, lb_logits, p, *, layer, final_norm, norm_f):
    ns, t, d = x3.shape
    m = ns * t
    x = x3.reshape(m, d)
    tm = min(m, 512)
    h = rmsnorm_bf16(x, p["norm1"], tm=min(m, 256))
    act = inproj(h, p["w_in"], p["norm_v"].reshape(1, -1), lb_logits, layer=layer, tm=tm, tn=512)
    width = act.shape[1] // 6
    span = min(t, GMLP_CHUNK)
    ya = gmlp(act, p["w_s"], p["b_s"], span=span, tb=min(m, 1024))
    blk = min(t, CHUNK)
    if t >= 512:
        sb, tb = 1, 2048
    else:
        sb, tb = ns, t
    yb3, s_new = hgrn(act.reshape(ns, t, -1), p["norm_o"], s0, blk=blk, sb=sb, tb=tb)
    x1 = outproj(x, ya, yb3.reshape(m, width), p["w_out"], tm=tm, tn=1024)
    x2 = mlp(x1, p["norm2"], p["w_up"], p["w_down"], norm_f, final_norm=final_norm, tm=tm, tf=512)
    v_act = act[:, width:2 * width].reshape(ns, t, width)
    return x2.reshape(ns, t, d), s_new, v_act


def kernel(x_prompt, x_sample, state_hgrn, norm1, w_in, w_s, b_s, norm_v, lb_logits, norm_o,
           w_out, norm2, w_up, w_down, norm_f):
    depth = w_in.shape[0]
    lb_logits = lb_logits.astype(F32)
    xp, xs = x_prompt, x_sample
    s_prompt, s_sample, v_sample = [], [], []
    for l in range(depth):
        p = dict(norm1=norm1[l], w_in=w_in[l].astype(BF16), w_s=w_s[l], b_s=b_s[l], norm_v=norm_v[l],
                 norm_o=norm_o[l], w_out=w_out[l].astype(BF16), norm2=norm2[l],
                 w_up=w_up[l].astype(BF16), w_down=w_down[l].astype(BF16))
        last = l == depth - 1
        xp, sp, _ = _trunk_layer(xp, None, lb_logits, p, layer=l, final_norm=last, norm_f=norm_f)
        xs, ss, vs = _trunk_layer(xs, state_hgrn[l], lb_logits, p, layer=l, final_norm=last, norm_f=norm_f)
        s_prompt.append(sp)
        s_sample.append(ss)
        v_sample.append(vs)
    return (xp, xs, jnp.stack(s_prompt), jnp.stack(s_sample), jnp.stack(v_sample))
```

```python
import functools

import jax
import jax.numpy as jnp
from jax import lax
from jax.experimental import pallas as pl
from jax.experimental.pallas import tpu as pltpu

EPS = 1e-6
CHUNK = 64
GMLP_CHUNK = 128
HEADS_A = 8
DK_B = 128
BF16 = jnp.bfloat16
F32 = jnp.float32

V7X_VMEM_BYTES = 64 * 1024 * 1024
VMEM_LIMIT_BYTES = V7X_VMEM_BYTES - 4 * 1024 * 1024


def _params(*semantics):
    return pltpu.CompilerParams(dimension_semantics=semantics, vmem_limit_bytes=VMEM_LIMIT_BYTES)


def _rms_scale(x):
    return lax.rsqrt(jnp.mean(x * x, axis=-1, keepdims=True) + EPS)


def _inproj_kernel(x_ref, n1_ref, wu_ref, wv_ref, wq_ref, wf_ref, wi_ref, wg_ref, nv_ref, lb_ref,
                   ws_ref, bs_ref, *refs, span, layer, emit_v):
    if emit_v:
        ya_ref, v_ref, q_ref, f_ref, i_ref, g_ref, h_ref = refs
    else:
        ya_ref, q_ref, f_ref, i_ref, g_ref, h_ref = refs

    @pl.when(pl.program_id(1) == 0)
    def _():
        x = x_ref[...]
        h_ref[...] = (x * _rms_scale(x) * n1_ref[...]).astype(h_ref.dtype)

    proj = lambda w_ref: jnp.dot(h_ref[...], w_ref[...], preferred_element_type=F32)

    u = jax.nn.gelu(proj(wu_ref))
    a = jax.nn.gelu(proj(wv_ref))
    v = a * _rms_scale(a) * nv_ref[...]
    if emit_v:
        v_ref[...] = v
    row = lax.broadcasted_iota(jnp.int32, (span, span), 0)
    col = lax.broadcasted_iota(jnp.int32, (span, span), 1)
    w = jnp.where(col // CHUNK <= row // CHUNK, ws_ref[:span, :span], 0.0).astype(BF16)
    bias = bs_ref[:span, :]
    for c in range(u.shape[0] // span):
        sl = slice(c * span, (c + 1) * span)
        mixed = jnp.dot(w, v[sl, :].astype(BF16), preferred_element_type=F32) + bias
        ya_ref[sl, :] = (u[sl, :] * mixed).astype(ya_ref.dtype)

    qz = proj(wq_ref)
    q_ref[...] = (qz * jax.nn.sigmoid(qz)).astype(q_ref.dtype)

    lg = lb_ref[...]
    e = jnp.exp(lg - jnp.max(lg, axis=0, keepdims=True))
    sm = e / jnp.sum(e, axis=0, keepdims=True)
    lb = jnp.sum(sm[:layer + 1, :], axis=0, keepdims=True)
    f_ref[...] = lb + (1.0 - lb) * jax.nn.sigmoid(proj(wf_ref))

    i_ref[...] = proj(wi_ref).astype(i_ref.dtype)

    gz = proj(wg_ref)
    g_ref[...] = (gz * jax.nn.sigmoid(gz)).astype(g_ref.dtype)


def inproj(x, norm1, w_in, norm_v, lb_logits, w_s, b_s, *, span, layer, emit_v, tm):
    m, d = x.shape
    width = w_in.shape[1] // 6
    tn = width // HEADS_A
    nt = width // tn
    w_spec = lambda g: pl.BlockSpec((d, tn), lambda i, j: (0, g * nt + j))
    tile = pl.BlockSpec((tm, tn), lambda i, j: (i, j))
    out_dtypes = [BF16] + ([F32] if emit_v else []) + [BF16, F32, BF16, BF16]
    kern = functools.partial(_inproj_kernel, span=span, layer=layer, emit_v=emit_v)
    return pl.pallas_call(
        kern,
        grid=(m // tm, nt),
        in_specs=[pl.BlockSpec((tm, d), lambda i, j: (i, 0)),
                  pl.BlockSpec((1, d), lambda i, j: (0, 0))]
                 + [w_spec(g) for g in range(6)]
                 + [pl.BlockSpec((1, tn), lambda i, j: (0, j)),
                    pl.BlockSpec((lb_logits.shape[0], tn), lambda i, j: (0, j)),
                    pl.BlockSpec((None, GMLP_CHUNK, GMLP_CHUNK), lambda i, j: (j, 0, 0)),
                    pl.BlockSpec((None, GMLP_CHUNK, 1), lambda i, j: (j, 0, 0))],
        out_specs=[tile] * len(out_dtypes),
        out_shape=[jax.ShapeDtypeStruct((m, width), dt) for dt in out_dtypes],
        scratch_shapes=[pltpu.VMEM((tm, d), BF16)],
        compiler_params=_params("parallel", "arbitrary"),
        name="inproj",
    )(x, norm1.reshape(1, d), *([w_in] * 6), norm_v.reshape(1, width), lb_logits,
      w_s, b_s.reshape(HEADS_A, GMLP_CHUNK, 1))


def _split2_bf16(x):
    hi = x.astype(BF16)
    lo = (x - hi.astype(F32)).astype(BF16)
    return hi, lo


def _hgrn_kernel(q_ref, f_ref, i_ref, g_ref, no_ref, s0_ref, y_ref, s_ref, st_ref, *, blk, has_s0):
    tb = pl.program_id(2)
    n_streams, tokens, _ = q_ref.shape
    n_chunks = tokens // blk
    items = [(s, c) for s in range(n_streams) for c in range(n_chunks)]
    rows = lambda c: slice(c * blk, (c + 1) * blk)

    @pl.when(tb == 0)
    def _():
        if has_s0:
            for s in range(n_streams):
                st_ref[s] = s0_ref[s].T
        else:
            st_ref[...] = jnp.zeros_like(st_ref)

    row = lax.broadcasted_iota(jnp.int32, (blk, blk), 0)
    col = lax.broadcasted_iota(jnp.int32, (blk, blk), 1)
    causal = col <= row
    tri = causal.astype(BF16)
    gain = no_ref[...]
    nt = (((1,), (1,)), ((), ()))
    tn = (((0,), (0,)), ((), ()))

    cum = {}
    for s, c in items:
        hi, lo = _split2_bf16(jnp.log(f_ref[s, rows(c), :]))
        cum[s, c] = (jnp.dot(tri, hi, preferred_element_type=F32)
                     + jnp.dot(tri, lo, preferred_element_type=F32))

    q_rel, k_rel, q_abs, k_end, decay, val = {}, {}, {}, {}, {}, {}
    for s, c in items:
        b = cum[s, c]
        b_mid = b[blk // 2:blk // 2 + 1, :]
        b_end = b[blk - 1:blk, :]
        qr = q_ref[s, rows(c), :] * jnp.exp(b - b_mid)
        kr = (1.0 - f_ref[s, rows(c), :]) * jnp.exp(b_mid - b)
        q_rel[s, c] = qr.astype(BF16)
        k_rel[s, c] = kr.astype(BF16)
        q_abs[s, c] = (qr * jnp.exp(b_mid)).astype(BF16)
        k_end[s, c] = (kr * jnp.exp(b_end - b_mid)).astype(BF16)
        decay[s, c] = jnp.exp(b_end)
        val[s, c] = i_ref[s, rows(c), :].astype(BF16)

    scores = {}
    for it in items:
        sc = lax.dot_general(q_rel[it], k_rel[it], nt, preferred_element_type=F32)
        scores[it] = jnp.where(causal, sc, 0.0).astype(BF16)

    out, delta = {}, {}
    for it in items:
        out[it] = jnp.dot(scores[it], val[it], preferred_element_type=F32)
        delta[it] = lax.dot_general(val[it], k_end[it], tn, preferred_element_type=F32)

    for s in range(n_streams):
        st = st_ref[s]
        for c in range(n_chunks):
            out[s, c] = out[s, c] + lax.dot_general(q_abs[s, c], st.astype(BF16), nt,
                                                    preferred_element_type=F32)
            st = st * decay[s, c] + delta[s, c]
        st_ref[s] = st

    for s, c in items:
        o = out[s, c]
        y_ref[s, rows(c), :] = (o * _rms_scale(o) * gain * g_ref[s, rows(c), :]).astype(y_ref.dtype)

    @pl.when(tb == pl.num_programs(2) - 1)
    def _():
        for s in range(n_streams):
            s_ref[s] = st_ref[s].T


def hgrn(q, f, i, g, norm_o, s0, *, blk, sb, tb):
    ns, t, width = q.shape
    heads = width // DK_B
    has_s0 = s0 is not None
    tok_spec = pl.BlockSpec((sb, tb, DK_B), lambda b, h, c: (b, c, h))
    state_spec = pl.BlockSpec((sb, None, DK_B, DK_B), lambda b, h, c: (b, h, 0, 0))
    if not has_s0:
        s0 = jnp.zeros((ns, heads, DK_B, DK_B), F32)
    kern = functools.partial(_hgrn_kernel, blk=blk, has_s0=has_s0)
    return pl.pallas_call(
        kern,
        grid=(ns // sb, heads, t // tb),
        in_specs=[tok_spec, tok_spec, tok_spec, tok_spec,
                  pl.BlockSpec((None, 1, DK_B), lambda b, h, c: (h, 0, 0)),
                  state_spec],
        out_specs=[tok_spec, state_spec],
        out_shape=[jax.ShapeDtypeStruct((ns, t, width), BF16),
                   jax.ShapeDtypeStruct((ns, heads, DK_B, DK_B), F32)],
        scratch_shapes=[pltpu.VMEM((sb, DK_B, DK_B), F32)],
        compiler_params=_params("parallel", "parallel", "arbitrary"),
        name="hgrn",
    )(q, f, i, g, norm_o.reshape(heads, 1, DK_B), s0)


def _outproj_kernel(x_ref, ya_ref, yb_ref, wa_ref, wb_ref, o_ref):
    o_ref[...] = (x_ref[...]
                  + jnp.dot(ya_ref[...], wa_ref[...], preferred_element_type=F32)
                  + jnp.dot(yb_ref[...], wb_ref[...], preferred_element_type=F32))


def outproj(x, ya, yb, w_out, *, tm, tn):
    m, d = x.shape
    ka = ya.shape[1]
    return pl.pallas_call(
        _outproj_kernel,
        grid=(m // tm, d // tn),
        in_specs=[pl.BlockSpec((tm, tn), lambda i, j: (i, j)),
                  pl.BlockSpec((tm, ka), lambda i, j: (i, 0)),
                  pl.BlockSpec((tm, ka), lambda i, j: (i, 0)),
                  pl.BlockSpec((ka, tn), lambda i, j: (0, j)),
                  pl.BlockSpec((ka, tn), lambda i, j: (1, j))],
        out_specs=pl.BlockSpec((tm, tn), lambda i, j: (i, j)),
        out_shape=jax.ShapeDtypeStruct((m, d), F32),
        compiler_params=_params("parallel", "arbitrary"),
        name="outproj",
    )(x, ya, yb, w_out, w_out)


def _mlp_kernel(x_ref, n2_ref, wu_ref, wd_ref, nf_ref, o_ref, h_ref, *, final_norm):
    f = pl.program_id(1)

    @pl.when(f == 0)
    def _():
        x = x_ref[...]
        h_ref[...] = (x * _rms_scale(x) * n2_ref[...]).astype(h_ref.dtype)
        o_ref[...] = x

    hid = jnp.dot(h_ref[...], wu_ref[...], preferred_element_type=F32)
    hid = jnp.square(jnp.maximum(hid, 0.0)).astype(BF16)
    o_ref[...] += jnp.dot(hid, wd_ref[...], preferred_element_type=F32)

    if final_norm:
        @pl.when(f == pl.num_programs(1) - 1)
        def _():
            x = o_ref[...]
            o_ref[...] = x * _rms_scale(x) * nf_ref[...]


def mlp(x, norm2, w_up, w_down, norm_f, *, final_norm, tm, tf):
    m, d = x.shape
    ff = w_up.shape[1]
    once = pl.Buffered(1)
    kern = functools.partial(_mlp_kernel, final_norm=final_norm)
    return pl.pallas_call(
        kern,
        grid=(m // tm, ff // tf),
        in_specs=[pl.BlockSpec((tm, d), lambda i, f: (i, 0), pipeline_mode=once),
                  pl.BlockSpec((1, d), lambda i, f: (0, 0), pipeline_mode=once),
                  pl.BlockSpec((d, tf), lambda i, f: (0, f)),
                  pl.BlockSpec((tf, d), lambda i, f: (f, 0)),
                  pl.BlockSpec((1, d), lambda i, f: (0, 0), pipeline_mode=once)],
        out_specs=pl.BlockSpec((tm, d), lambda i, f: (i, 0)),
        out_shape=jax.ShapeDtypeStruct((m, d), F32),
        scratch_shapes=[pltpu.VMEM((tm, d), BF16)],
        compiler_params=_params("parallel", "arbitrary"),
        name="mlp",
    )(x, norm2.reshape(1, d), w_up, w_down, norm_f.reshape(1, d))


def _trunk_layer(x3, s0, lb_logits, p, *, layer, final_norm, norm_f, emit_v):
    ns, t, d = x3.shape
    m = ns * t
    x = x3.reshape(m, d)
    tm = min(m, 512)
    outs = inproj(x, p["norm1"], p["w_in"], p["norm_v"], lb_logits, p["w_s"], p["b_s"],
                  span=min(t, GMLP_CHUNK), layer=layer, emit_v=emit_v, tm=tm)
    ya, (q, f, i, g) = outs[0], outs[-4:]
    width = ya.shape[1]
    if t >= 2048:
        sb, tb = 1, 2048
    else:
        sb, tb = ns, t
    per_stream = lambda a: a.reshape(ns, t, width)
    yb3, s_new = hgrn(per_stream(q), per_stream(f), per_stream(i), per_stream(g), p["norm_o"], s0,
                      blk=min(t, CHUNK), sb=sb, tb=tb)
    x1 = outproj(x, ya, yb3.reshape(m, width), p["w_out"], tm=tm, tn=1024)
    x2 = mlp(x1, p["norm2"], p["w_up"], p["w_down"], norm_f, final_norm=final_norm, tm=tm, tf=512)
    v_act = per_stream(outs[1]) if emit_v else None
    return x2.reshape(ns, t, d), s_new, v_act


def kernel(x_prompt, x_sample, state_hgrn, norm1, w_in, w_s, b_s, norm_v, lb_logits, norm_o,
           w_out, norm2, w_up, w_down, norm_f):
    depth = w_in.shape[0]
    lb_logits = lb_logits.astype(F32)
    xp, xs = x_prompt, x_sample
    s_prompt, s_sample, v_sample = [], [], []
    for l in range(depth):
        p = dict(norm1=norm1[l], w_in=w_in[l].astype(BF16), w_s=w_s[l], b_s=b_s[l], norm_v=norm_v[l],
                 norm_o=norm_o[l], w_out=w_out[l].astype(BF16), norm2=norm2[l],
                 w_up=w_up[l].astype(BF16), w_down=w_down[l].astype(BF16))
        last = l == depth - 1
        xp, sp, _ = _trunk_layer(xp, None, lb_logits, p, layer=l, final_norm=last, norm_f=norm_f,
                                 emit_v=False)
        xs, ss, vs = _trunk_layer(xs, state_hgrn[l], lb_logits, p, layer=l, final_norm=last, norm_f=norm_f,
                                  emit_v=True)
        s_prompt.append(sp)
        s_sample.append(ss)
        v_sample.append(vs)
    return (xp, xs, jnp.stack(s_prompt), jnp.stack(s_sample), jnp.stack(v_sample))
```

```python
import functools

import jax
import jax.numpy as jnp
from jax import lax
from jax.experimental import pallas as pl
from jax.experimental.pallas import tpu as pltpu

EPS = 1e-6
CHUNK = 64
GMLP_CHUNK = 128
HEADS_A = 8
N_GROUPS = 6
DK_B = 128
MLP_ACC_COLS = 1024
NORM_ROWS = 16
BF16 = jnp.bfloat16
F32 = jnp.float32

V7X_VMEM_BYTES = 64 * 1024 * 1024
VMEM_LIMIT_BYTES = V7X_VMEM_BYTES - 4 * 1024 * 1024


def _params(*semantics):
    return pltpu.CompilerParams(dimension_semantics=semantics, vmem_limit_bytes=VMEM_LIMIT_BYTES)


def _rms_scale(x):
    return lax.rsqrt(jnp.mean(x * x, axis=-1, keepdims=True) + EPS)


def _for_row_chunks(n_rows, body):
    for r in range(0, n_rows, NORM_ROWS):
        body(slice(r, r + NORM_ROWS))


def _inproj_epilogue(pre, nv_ref, lb_ref, ws_ref, bs_ref, out_refs, *, span, layer, emit_v):
    if emit_v:
        ya_ref, v_ref, q_ref, f_ref, i_ref, g_ref = out_refs
    else:
        ya_ref, q_ref, f_ref, i_ref, g_ref = out_refs

    u = jax.nn.gelu(pre(0))
    a = jax.nn.gelu(pre(1))
    v = a * _rms_scale(a) * nv_ref[...]
    if emit_v:
        v_ref[...] = v
    row = lax.broadcasted_iota(jnp.int32, (span, span), 0)
    col = lax.broadcasted_iota(jnp.int32, (span, span), 1)
    w = jnp.where(col // CHUNK <= row // CHUNK, ws_ref[:span, :span], 0.0).astype(BF16)
    bias = bs_ref[:span, :]
    for c in range(u.shape[0] // span):
        sl = slice(c * span, (c + 1) * span)
        mixed = jnp.dot(w, v[sl, :].astype(BF16), preferred_element_type=F32) + bias
        ya_ref[sl, :] = (u[sl, :] * mixed).astype(ya_ref.dtype)

    qz = pre(2)
    q_ref[...] = (qz * jax.nn.sigmoid(qz)).astype(q_ref.dtype)

    lg = lb_ref[...]
    e = jnp.exp(lg - jnp.max(lg, axis=0, keepdims=True))
    sm = e / jnp.sum(e, axis=0, keepdims=True)
    lb = jnp.sum(sm[:layer + 1, :], axis=0, keepdims=True)
    f_ref[...] = lb + (1.0 - lb) * jax.nn.sigmoid(pre(3))

    i_ref[...] = pre(4).astype(i_ref.dtype)

    gz = pre(5)
    g_ref[...] = (gz * jax.nn.sigmoid(gz)).astype(g_ref.dtype)


def _inproj_kernel(x_ref, n1_ref, *refs, span, layer, emit_v):
    w_refs = refs[:N_GROUPS]
    nv_ref, lb_ref, ws_ref, bs_ref = refs[N_GROUPS:N_GROUPS + 4]
    out_refs, h_ref = refs[N_GROUPS + 4:-1], refs[-1]

    @pl.when(pl.program_id(1) == 0)
    def _():
        def norm_rows(rs):
            x = x_ref[rs, :]
            h_ref[rs, :] = (x * _rms_scale(x) * n1_ref[...]).astype(h_ref.dtype)
        _for_row_chunks(x_ref.shape[0], norm_rows)

    pre = lambda g: jnp.dot(h_ref[...], w_refs[g][...], preferred_element_type=F32)
    _inproj_epilogue(pre, nv_ref, lb_ref, ws_ref, bs_ref, out_refs, span=span, layer=layer, emit_v=emit_v)


def _inproj_convert_kernel(x_ref, n1_ref, *refs, span, layer, emit_v):
    w_refs = refs[:N_GROUPS]
    nv_ref, lb_ref, ws_ref, bs_ref = refs[N_GROUPS:N_GROUPS + 4]
    n_act = 6 if emit_v else 5
    out_refs = refs[N_GROUPS + 4:N_GROUPS + 4 + n_act]
    wb_refs = refs[N_GROUPS + 4 + n_act:-2]
    h_ref, acc_ref = refs[-2:]
    j, k = pl.program_id(1), pl.program_id(2)
    kt, _, tk = h_ref.shape

    @pl.when(jnp.logical_and(j == 0, k == 0))
    def _():
        def norm_rows(rs):
            x = x_ref[rs, :]
            h = (x * _rms_scale(x) * n1_ref[...]).astype(h_ref.dtype)
            for kk in range(kt):
                h_ref[kk, rs, :] = h[:, kk * tk:(kk + 1) * tk]
        _for_row_chunks(x_ref.shape[0], norm_rows)

    for g in range(N_GROUPS):
        wb = w_refs[g][...].astype(BF16)
        wb_refs[g][...] = wb
        part = jnp.dot(h_ref[k], wb, preferred_element_type=F32)

        @pl.when(k == 0)
        def _():
            acc_ref[g] = part

        @pl.when(k > 0)
        def _():
            acc_ref[g] += part

    @pl.when(k == kt - 1)
    def _():
        _inproj_epilogue(lambda g: acc_ref[g], nv_ref, lb_ref, ws_ref, bs_ref, out_refs,
                         span=span, layer=layer, emit_v=emit_v)


def inproj(x, norm1, w_groups, norm_v, lb_logits, w_s, b_s, *, span, layer, emit_v, tm):
    m, d = x.shape
    width = w_groups[0].shape[1]
    tn = width // HEADS_A
    tile = pl.BlockSpec((tm, tn), lambda i, j: (i, j))
    out_dtypes = [BF16] + ([F32] if emit_v else []) + [BF16, F32, BF16, BF16]
    kern = functools.partial(_inproj_kernel, span=span, layer=layer, emit_v=emit_v)
    return pl.pallas_call(
        kern,
        grid=(m // tm, width // tn),
        in_specs=[pl.BlockSpec((tm, d), lambda i, j: (i, 0)),
                  pl.BlockSpec((1, d), lambda i, j: (0, 0))]
                 + [pl.BlockSpec((d, tn), lambda i, j: (0, j))] * N_GROUPS
                 + [pl.BlockSpec((1, tn), lambda i, j: (0, j)),
                    pl.BlockSpec((lb_logits.shape[0], tn), lambda i, j: (0, j)),
                    pl.BlockSpec((None, GMLP_CHUNK, GMLP_CHUNK), lambda i, j: (j, 0, 0)),
                    pl.BlockSpec((None, GMLP_CHUNK, 1), lambda i, j: (j, 0, 0))],
        out_specs=[tile] * len(out_dtypes),
        out_shape=[jax.ShapeDtypeStruct((m, width), dt) for dt in out_dtypes],
        scratch_shapes=[pltpu.VMEM((tm, d), BF16)],
        compiler_params=_params("parallel", "arbitrary"),
        name="inproj",
    )(x, norm1.reshape(1, d), *w_groups, norm_v.reshape(1, width), lb_logits,
      w_s, b_s.reshape(HEADS_A, GMLP_CHUNK, 1))


def inproj_convert(x, norm1, w_in, norm_v, lb_logits, w_s, b_s, *, span, layer, emit_v, tk):
    m, d = x.shape
    width = w_in.shape[1] // N_GROUPS
    tn = width // HEADS_A
    nt, kt = width // tn, d // tk
    w_spec = lambda g: pl.BlockSpec((tk, tn), lambda i, j, k: (k, g * nt + j))
    tile = pl.BlockSpec((m, tn), lambda i, j, k: (i, j))
    out_dtypes = [BF16] + ([F32] if emit_v else []) + [BF16, F32, BF16, BF16]
    kern = functools.partial(_inproj_convert_kernel, span=span, layer=layer, emit_v=emit_v)
    outs = pl.pallas_call(
        kern,
        grid=(1, nt, kt),
        in_specs=[pl.BlockSpec((m, d), lambda i, j, k: (i, 0)),
                  pl.BlockSpec((1, d), lambda i, j, k: (0, 0))]
                 + [w_spec(g) for g in range(N_GROUPS)]
                 + [pl.BlockSpec((1, tn), lambda i, j, k: (0, j)),
                    pl.BlockSpec((lb_logits.shape[0], tn), lambda i, j, k: (0, j)),
                    pl.BlockSpec((None, GMLP_CHUNK, GMLP_CHUNK), lambda i, j, k: (j, 0, 0)),
                    pl.BlockSpec((None, GMLP_CHUNK, 1), lambda i, j, k: (j, 0, 0))],
        out_specs=[tile] * len(out_dtypes)
                  + [pl.BlockSpec((tk, tn), lambda i, j, k: (k, j))] * N_GROUPS,
        out_shape=[jax.ShapeDtypeStruct((m, width), dt) for dt in out_dtypes]
                  + [jax.ShapeDtypeStruct((d, width), BF16)] * N_GROUPS,
        scratch_shapes=[pltpu.VMEM((kt, m, tk), BF16), pltpu.VMEM((N_GROUPS, m, tn), F32)],
        compiler_params=_params("arbitrary", "arbitrary", "arbitrary"),
        name="inproj_convert",
    )(x, norm1.reshape(1, d), *([w_in] * N_GROUPS), norm_v.reshape(1, width), lb_logits,
      w_s, b_s.reshape(HEADS_A, GMLP_CHUNK, 1))
    return outs[:len(out_dtypes)], outs[len(out_dtypes):]


def _split2_bf16(x):
    hi = x.astype(BF16)
    lo = (x - hi.astype(F32)).astype(BF16)
    return hi, lo


def _hgrn_kernel(q_ref, f_ref, i_ref, g_ref, no_ref, s0_ref, y_ref, s_ref, st_ref, *, blk, has_s0):
    tb = pl.program_id(2)
    n_streams, tokens, _ = q_ref.shape
    n_chunks = tokens // blk
    items = [(s, c) for s in range(n_streams) for c in range(n_chunks)]
    rows = lambda c: slice(c * blk, (c + 1) * blk)

    @pl.when(tb == 0)
    def _():
        if has_s0:
            for s in range(n_streams):
                st_ref[s] = s0_ref[s].T
        else:
            st_ref[...] = jnp.zeros_like(st_ref)

    row = lax.broadcasted_iota(jnp.int32, (blk, blk), 0)
    col = lax.broadcasted_iota(jnp.int32, (blk, blk), 1)
    causal = col <= row
    tri = causal.astype(BF16)
    gain = no_ref[...]
    nt = (((1,), (1,)), ((), ()))
    tn = (((0,), (0,)), ((), ()))

    cum = {}
    for s, c in items:
        hi, lo = _split2_bf16(jnp.log(f_ref[s, rows(c), :]))
        cum[s, c] = (jnp.dot(tri, hi, preferred_element_type=F32)
                     + jnp.dot(tri, lo, preferred_element_type=F32))

    q_rel, k_rel, q_abs, k_end, decay, val = {}, {}, {}, {}, {}, {}
    for s, c in items:
        b = cum[s, c]
        b_mid = b[blk // 2:blk // 2 + 1, :]
        b_end = b[blk - 1:blk, :]
        qr = q_ref[s, rows(c), :] * jnp.exp(b - b_mid)
        kr = (1.0 - f_ref[s, rows(c), :]) * jnp.exp(b_mid - b)
        q_rel[s, c] = qr.astype(BF16)
        k_rel[s, c] = kr.astype(BF16)
        q_abs[s, c] = (qr * jnp.exp(b_mid)).astype(BF16)
        k_end[s, c] = (kr * jnp.exp(b_end - b_mid)).astype(BF16)
        decay[s, c] = jnp.exp(b_end)
        val[s, c] = i_ref[s, rows(c), :].astype(BF16)

    scores = {}
    for it in items:
        sc = lax.dot_general(q_rel[it], k_rel[it], nt, preferred_element_type=F32)
        scores[it] = jnp.where(causal, sc, 0.0).astype(BF16)

    out, delta = {}, {}
    for it in items:
        out[it] = jnp.dot(scores[it], val[it], preferred_element_type=F32)
        delta[it] = lax.dot_general(val[it], k_end[it], tn, preferred_element_type=F32)

    for s in range(n_streams):
        st = st_ref[s]
        for c in range(n_chunks):
            out[s, c] = out[s, c] + lax.dot_general(q_abs[s, c], st.astype(BF16), nt,
                                                    preferred_element_type=F32)
            st = st * decay[s, c] + delta[s, c]
        st_ref[s] = st

    for s, c in items:
        o = out[s, c]
        y_ref[s, rows(c), :] = (o * _rms_scale(o) * gain * g_ref[s, rows(c), :]).astype(y_ref.dtype)

    @pl.when(tb == pl.num_programs(2) - 1)
    def _():
        for s in range(n_streams):
            s_ref[s] = st_ref[s].T


def hgrn(q, f, i, g, norm_o, s0, *, blk, sb, tb):
    ns, t, width = q.shape
    heads = width // DK_B
    has_s0 = s0 is not None
    tok_spec = pl.BlockSpec((sb, tb, DK_B), lambda b, h, c: (b, c, h))
    state_spec = pl.BlockSpec((sb, None, DK_B, DK_B), lambda b, h, c: (b, h, 0, 0))
    if not has_s0:
        s0 = jnp.zeros((ns, heads, DK_B, DK_B), F32)
    kern = functools.partial(_hgrn_kernel, blk=blk, has_s0=has_s0)
    return pl.pallas_call(
        kern,
        grid=(ns // sb, heads, t // tb),
        in_specs=[tok_spec, tok_spec, tok_spec, tok_spec,
                  pl.BlockSpec((None, 1, DK_B), lambda b, h, c: (h, 0, 0)),
                  state_spec],
        out_specs=[tok_spec, state_spec],
        out_shape=[jax.ShapeDtypeStruct((ns, t, width), BF16),
                   jax.ShapeDtypeStruct((ns, heads, DK_B, DK_B), F32)],
        scratch_shapes=[pltpu.VMEM((sb, DK_B, DK_B), F32)],
        compiler_params=_params("parallel", "parallel", "arbitrary"),
        name="hgrn",
    )(q, f, i, g, norm_o.reshape(heads, 1, DK_B), s0)


def _outproj_kernel(x_ref, ya_ref, yb_ref, wa_ref, wb_ref, o_ref, *wcopy_refs):
    wa, wb = wa_ref[...], wb_ref[...]
    if wcopy_refs:
        wa, wb = wa.astype(BF16), wb.astype(BF16)
        wcopy_refs[0][...] = wa
        wcopy_refs[1][...] = wb
    o_ref[...] = (x_ref[...]
                  + jnp.dot(ya_ref[...], wa, preferred_element_type=F32)
                  + jnp.dot(yb_ref[...], wb, preferred_element_type=F32))


def outproj(x, ya, yb, w_a, w_b, b_block, *, convert, tm, tn):
    m, d = x.shape
    ka = ya.shape[1]
    w_tile = lambda blk: pl.BlockSpec((ka, tn), lambda i, j: (blk, j))
    out_specs = [pl.BlockSpec((tm, tn), lambda i, j: (i, j))]
    out_shape = [jax.ShapeDtypeStruct((m, d), F32)]
    if convert:
        assert m == tm, "weight copies are written once per row block"
        out_specs += [w_tile(0)] * 2
        out_shape += [jax.ShapeDtypeStruct((ka, d), BF16)] * 2
    return pl.pallas_call(
        _outproj_kernel,
        grid=(m // tm, d // tn),
        in_specs=[pl.BlockSpec((tm, tn), lambda i, j: (i, j)),
                  pl.BlockSpec((tm, ka), lambda i, j: (i, 0)),
                  pl.BlockSpec((tm, ka), lambda i, j: (i, 0)),
                  w_tile(0), w_tile(b_block)],
        out_specs=out_specs,
        out_shape=out_shape,
        compiler_params=_params("parallel", "arbitrary"),
        name="outproj_convert" if convert else "outproj",
    )(x, ya, yb, w_a, w_b)


def _mlp_kernel(x_ref, n2_ref, wu_ref, wd_ref, nf_ref, o_ref, *refs, final_norm):
    h_ref = refs[-1]
    f = pl.program_id(1)

    @pl.when(f == 0)
    def _():
        def norm_rows(rs):
            x = x_ref[rs, :]
            h_ref[rs, :] = (x * _rms_scale(x) * n2_ref[...]).astype(h_ref.dtype)
            o_ref[rs, :] = x
        _for_row_chunks(x_ref.shape[0], norm_rows)

    wu, wd = wu_ref[...], wd_ref
    if len(refs) > 1:
        wu = wu.astype(BF16)
        refs[0][...] = wu
        refs[1][...] = wd_ref[...].astype(BF16)
        wd = refs[1]
    hid = jnp.dot(h_ref[...], wu, preferred_element_type=F32)
    hid = jnp.square(jnp.maximum(hid, 0.0)).astype(BF16)
    for n in range(o_ref.shape[1] // MLP_ACC_COLS):
        sl = slice(n * MLP_ACC_COLS, (n + 1) * MLP_ACC_COLS)
        o_ref[:, sl] += jnp.dot(hid, wd[:, sl], preferred_element_type=F32)

    if final_norm:
        @pl.when(f == pl.num_programs(1) - 1)
        def _():
            def norm_rows(rs):
                x = o_ref[rs, :]
                o_ref[rs, :] = x * _rms_scale(x) * nf_ref[...]
            _for_row_chunks(o_ref.shape[0], norm_rows)


def mlp(x, norm2, w_up, w_down, norm_f, *, final_norm, convert, tm, tf):
    m, d = x.shape
    ff = w_up.shape[1]
    once = pl.Buffered(1)
    up_tile = pl.BlockSpec((d, tf), lambda i, f: (0, f))
    down_tile = pl.BlockSpec((tf, d), lambda i, f: (f, 0))
    out_specs = [pl.BlockSpec((tm, d), lambda i, f: (i, 0))]
    out_shape = [jax.ShapeDtypeStruct((m, d), F32)]
    if convert:
        assert m == tm, "weight copies are written once per row block"
        out_specs += [up_tile, down_tile]
        out_shape += [jax.ShapeDtypeStruct((d, ff), BF16), jax.ShapeDtypeStruct((ff, d), BF16)]
    kern = functools.partial(_mlp_kernel, final_norm=final_norm)
    return pl.pallas_call(
        kern,
        grid=(m // tm, ff // tf),
        in_specs=[pl.BlockSpec((tm, d), lambda i, f: (i, 0), pipeline_mode=once),
                  pl.BlockSpec((1, d), lambda i, f: (0, 0), pipeline_mode=once),
                  up_tile, down_tile,
                  pl.BlockSpec((1, d), lambda i, f: (0, 0), pipeline_mode=once)],
        out_specs=out_specs,
        out_shape=out_shape,
        scratch_shapes=[pltpu.VMEM((tm, d), BF16)],
        compiler_params=_params("parallel", "arbitrary"),
        name="mlp_convert" if convert else "mlp",
    )(x, norm2.reshape(1, d), w_up, w_down, norm_f.reshape(1, d))


def _trunk_layer(x3, s0, lb_logits, p, *, layer, final_norm, norm_f, emit_v, convert):
    ns, t, d = x3.shape
    m = ns * t
    x = x3.reshape(m, d)
    tm = min(m, 512)
    span = min(t, GMLP_CHUNK)
    small = dict(norm_v=p["norm_v"], lb_logits=lb_logits, w_s=p["w_s"], b_s=p["b_s"])
    bf = {}
    if convert:
        assert m == tm
        outs, bf["w_in"] = inproj_convert(x, p["norm1"], p["w_in"], **small, span=span, layer=layer,
                                          emit_v=emit_v, tk=1024)
    else:
        outs = inproj(x, p["norm1"], p["w_in"], **small, span=span, layer=layer, emit_v=emit_v, tm=tm)
    ya, (q, f, i, g) = outs[0], outs[-4:]
    width = ya.shape[1]
    if t >= 2048:
        sb, tb = 1, 2048
    else:
        sb, tb = ns, t
    per_stream = lambda a: a.reshape(ns, t, width)
    yb3, s_new = hgrn(per_stream(q), per_stream(f), per_stream(i), per_stream(g), p["norm_o"], s0,
                      blk=min(t, CHUNK), sb=sb, tb=tb)
    yb = yb3.reshape(m, width)
    if convert:
        x1, *bf["w_out"] = outproj(x, ya, yb, p["w_out"], p["w_out"], 1, convert=True, tm=tm, tn=512)
        x2, bf["w_up"], bf["w_down"] = mlp(x1, p["norm2"], p["w_up"], p["w_down"], norm_f,
                                           final_norm=final_norm, convert=True, tm=tm, tf=256)
    else:
        x1, = outproj(x, ya, yb, *p["w_out"], 0, convert=False, tm=tm, tn=1024)
        x2, = mlp(x1, p["norm2"], p["w_up"], p["w_down"], norm_f, final_norm=final_norm, convert=False,
                  tm=tm, tf=512)
    v_act = per_stream(outs[1]) if emit_v else None
    return x2.reshape(ns, t, d), s_new, v_act, bf


def kernel(x_prompt, x_sample, state_hgrn, norm1, w_in, w_s, b_s, norm_v, lb_logits, norm_o,
           w_out, norm2, w_up, w_down, norm_f):
    depth = w_in.shape[0]
    lb_logits = lb_logits.astype(F32)
    xp, xs = x_prompt, x_sample
    s_prompt, s_sample, v_sample = [], [], []
    for l in range(depth):
        p = dict(norm1=norm1[l], w_in=w_in[l], w_s=w_s[l], b_s=b_s[l], norm_v=norm_v[l],
                 norm_o=norm_o[l], w_out=w_out[l], norm2=norm2[l], w_up=w_up[l], w_down=w_down[l])
        common = dict(layer=l, final_norm=l == depth - 1, norm_f=norm_f)
        xs, ss, vs, bf = _trunk_layer(xs, state_hgrn[l], lb_logits, p, emit_v=True, convert=True, **common)
        xp, sp, _, _ = _trunk_layer(xp, None, lb_logits, {**p, **bf}, emit_v=False, convert=False, **common)
        s_prompt.append(sp)
        s_sample.append(ss)
        v_sample.append(vs)
    return (xp, xs, jnp.stack(s_prompt), jnp.stack(s_sample), jnp.stack(v_sample))
```

```python
import functools

import jax
import jax.numpy as jnp
from jax import lax
from jax.experimental import pallas as pl
from jax.experimental.pallas import tpu as pltpu

EPS = 1e-6
CHUNK = 64
GMLP_CHUNK = 128
HEADS_A = 8
N_GROUPS = 6
DK_B = 128
MLP_ACC_COLS = 1024
NORM_ROWS = 16
BF16 = jnp.bfloat16
F32 = jnp.float32

V7X_VMEM_BYTES = 64 * 1024 * 1024
VMEM_LIMIT_BYTES = V7X_VMEM_BYTES - 4 * 1024 * 1024


def _params(*semantics):
    return pltpu.CompilerParams(dimension_semantics=semantics, vmem_limit_bytes=VMEM_LIMIT_BYTES)


def _rms_scale(x):
    return lax.rsqrt(jnp.mean(x * x, axis=-1, keepdims=True) + EPS)


def _for_row_chunks(n_rows, body):
    for r in range(0, n_rows, NORM_ROWS):
        body(slice(r, r + NORM_ROWS))


def _inproj_epilogue(pre, nv_ref, lb_ref, ws_ref, bs_ref, out_refs, *, span, layer, emit_v):
    if emit_v:
        ya_ref, v_ref, q_ref, f_ref, i_ref, g_ref = out_refs
    else:
        ya_ref, q_ref, f_ref, i_ref, g_ref = out_refs

    u = jax.nn.gelu(pre(0))
    a = jax.nn.gelu(pre(1))
    v = a * _rms_scale(a) * nv_ref[...]
    if emit_v:
        v_ref[...] = v
    row = lax.broadcasted_iota(jnp.int32, (span, span), 0)
    col = lax.broadcasted_iota(jnp.int32, (span, span), 1)
    w = jnp.where(col // CHUNK <= row // CHUNK, ws_ref[:span, :span], 0.0).astype(BF16)
    bias = bs_ref[:span, :]
    for c in range(u.shape[0] // span):
        sl = slice(c * span, (c + 1) * span)
        mixed = jnp.dot(w, v[sl, :].astype(BF16), preferred_element_type=F32) + bias
        ya_ref[sl, :] = (u[sl, :] * mixed).astype(ya_ref.dtype)

    qz = pre(2)
    q_ref[...] = (qz * jax.nn.sigmoid(qz)).astype(q_ref.dtype)

    lg = lb_ref[...]
    e = jnp.exp(lg - jnp.max(lg, axis=0, keepdims=True))
    sm = e / jnp.sum(e, axis=0, keepdims=True)
    lb = jnp.sum(sm[:layer + 1, :], axis=0, keepdims=True)
    f_ref[...] = lb + (1.0 - lb) * jax.nn.sigmoid(pre(3))

    i_ref[...] = pre(4).astype(i_ref.dtype)

    gz = pre(5)
    g_ref[...] = (gz * jax.nn.sigmoid(gz)).astype(g_ref.dtype)


def _inproj_kernel(x_ref, n1_ref, *refs, span, layer, emit_v):
    w_refs = refs[:N_GROUPS]
    nv_ref, lb_ref, ws_ref, bs_ref = refs[N_GROUPS:N_GROUPS + 4]
    out_refs, h_ref = refs[N_GROUPS + 4:-1], refs[-1]

    @pl.when(pl.program_id(1) == 0)
    def _():
        def norm_rows(rs):
            x = x_ref[rs, :]
            h_ref[rs, :] = (x * _rms_scale(x) * n1_ref[...]).astype(h_ref.dtype)
        _for_row_chunks(x_ref.shape[0], norm_rows)

    pre = lambda g: jnp.dot(h_ref[...], w_refs[g][...], preferred_element_type=F32)
    _inproj_epilogue(pre, nv_ref, lb_ref, ws_ref, bs_ref, out_refs, span=span, layer=layer, emit_v=emit_v)


def _inproj_convert_kernel(x_ref, n1_ref, *refs, span, layer, emit_v):
    w_refs = refs[:N_GROUPS]
    nv_ref, lb_ref, ws_ref, bs_ref = refs[N_GROUPS:N_GROUPS + 4]
    n_act = 6 if emit_v else 5
    out_refs = refs[N_GROUPS + 4:N_GROUPS + 4 + n_act]
    wb_refs = refs[N_GROUPS + 4 + n_act:-2]
    h_ref, acc_ref = refs[-2:]
    j, k = pl.program_id(1), pl.program_id(2)
    kt, _, tk = h_ref.shape

    @pl.when(jnp.logical_and(j == 0, k == 0))
    def _():
        def norm_rows(rs):
            x = x_ref[rs, :]
            h = (x * _rms_scale(x) * n1_ref[...]).astype(h_ref.dtype)
            for kk in range(kt):
                h_ref[kk, rs, :] = h[:, kk * tk:(kk + 1) * tk]
        _for_row_chunks(x_ref.shape[0], norm_rows)

    @pl.when(k == 0)
    def _():
        acc_ref[...] = jnp.zeros_like(acc_ref)

    for g in range(N_GROUPS):
        wb = w_refs[g][...].astype(BF16)
        wb_refs[g][...] = wb
        acc_ref[g] += jnp.dot(h_ref[k], wb, preferred_element_type=F32)

    @pl.when(k == kt - 1)
    def _():
        _inproj_epilogue(lambda g: acc_ref[g], nv_ref, lb_ref, ws_ref, bs_ref, out_refs,
                         span=span, layer=layer, emit_v=emit_v)


def inproj(x, norm1, w_groups, norm_v, lb_logits, w_s, b_s, *, span, layer, emit_v, tm):
    m, d = x.shape
    width = w_groups[0].shape[1]
    tn = width // HEADS_A
    tile = pl.BlockSpec((tm, tn), lambda i, j: (i, j))
    out_dtypes = [BF16] + ([F32] if emit_v else []) + [BF16, F32, BF16, BF16]
    kern = functools.partial(_inproj_kernel, span=span, layer=layer, emit_v=emit_v)
    return pl.pallas_call(
        kern,
        grid=(m // tm, width // tn),
        in_specs=[pl.BlockSpec((tm, d), lambda i, j: (i, 0)),
                  pl.BlockSpec((1, d), lambda i, j: (0, 0))]
                 + [pl.BlockSpec((d, tn), lambda i, j: (0, j))] * N_GROUPS
                 + [pl.BlockSpec((1, tn), lambda i, j: (0, j)),
                    pl.BlockSpec((lb_logits.shape[0], tn), lambda i, j: (0, j)),
                    pl.BlockSpec((None, GMLP_CHUNK, GMLP_CHUNK), lambda i, j: (j, 0, 0)),
                    pl.BlockSpec((None, GMLP_CHUNK, 1), lambda i, j: (j, 0, 0))],
        out_specs=[tile] * len(out_dtypes),
        out_shape=[jax.ShapeDtypeStruct((m, width), dt) for dt in out_dtypes],
        scratch_shapes=[pltpu.VMEM((tm, d), BF16)],
        compiler_params=_params("parallel", "arbitrary"),
        name="inproj",
    )(x, norm1.reshape(1, d), *w_groups, norm_v.reshape(1, width), lb_logits,
      w_s, b_s.reshape(HEADS_A, GMLP_CHUNK, 1))


def inproj_convert(x, norm1, w_in, norm_v, lb_logits, w_s, b_s, *, span, layer, emit_v, tk):
    m, d = x.shape
    width = w_in.shape[1] // N_GROUPS
    tn = width // HEADS_A
    nt, kt = width // tn, d // tk
    w_spec = lambda g: pl.BlockSpec((tk, tn), lambda i, j, k: (k, g * nt + j))
    tile = pl.BlockSpec((m, tn), lambda i, j, k: (i, j))
    out_dtypes = [BF16] + ([F32] if emit_v else []) + [BF16, F32, BF16, BF16]
    kern = functools.partial(_inproj_convert_kernel, span=span, layer=layer, emit_v=emit_v)
    outs = pl.pallas_call(
        kern,
        grid=(1, nt, kt),
        in_specs=[pl.BlockSpec((m, d), lambda i, j, k: (i, 0)),
                  pl.BlockSpec((1, d), lambda i, j, k: (0, 0))]
                 + [w_spec(g) for g in range(N_GROUPS)]
                 + [pl.BlockSpec((1, tn), lambda i, j, k: (0, j)),
                    pl.BlockSpec((lb_logits.shape[0], tn), lambda i, j, k: (0, j)),
                    pl.BlockSpec((None, GMLP_CHUNK, GMLP_CHUNK), lambda i, j, k: (j, 0, 0)),
                    pl.BlockSpec((None, GMLP_CHUNK, 1), lambda i, j, k: (j, 0, 0))],
        out_specs=[tile] * len(out_dtypes)
                  + [pl.BlockSpec((tk, tn), lambda i, j, k: (k, j))] * N_GROUPS,
        out_shape=[jax.ShapeDtypeStruct((m, width), dt) for dt in out_dtypes]
                  + [jax.ShapeDtypeStruct((d, width), BF16)] * N_GROUPS,
        scratch_shapes=[pltpu.VMEM((kt, m, tk), BF16), pltpu.VMEM((N_GROUPS, m, tn), F32)],
        compiler_params=_params("arbitrary", "arbitrary", "arbitrary"),
        name="inproj_convert",
    )(x, norm1.reshape(1, d), *([w_in] * N_GROUPS), norm_v.reshape(1, width), lb_logits,
      w_s, b_s.reshape(HEADS_A, GMLP_CHUNK, 1))
    return outs[:len(out_dtypes)], outs[len(out_dtypes):]


def _split2_bf16(x):
    hi = x.astype(BF16)
    lo = (x - hi.astype(F32)).astype(BF16)
    return hi, lo


def _hgrn_kernel(q_ref, f_ref, i_ref, g_ref, no_ref, s0_ref, y_ref, s_ref, st_ref, *, blk, has_s0):
    tb = pl.program_id(2)
    n_streams, tokens, _ = q_ref.shape
    n_chunks = tokens // blk
    items = [(s, c) for s in range(n_streams) for c in range(n_chunks)]
    rows = lambda c: slice(c * blk, (c + 1) * blk)

    @pl.when(tb == 0)
    def _():
        if has_s0:
            for s in range(n_streams):
                st_ref[s] = s0_ref[s].T
        else:
            st_ref[...] = jnp.zeros_like(st_ref)

    row = lax.broadcasted_iota(jnp.int32, (blk, blk), 0)
    col = lax.broadcasted_iota(jnp.int32, (blk, blk), 1)
    causal = col <= row
    tri = causal.astype(BF16)
    gain = no_ref[...]
    nt = (((1,), (1,)), ((), ()))
    tn = (((0,), (0,)), ((), ()))

    cum = {}
    for s, c in items:
        hi, lo = _split2_bf16(jnp.log(f_ref[s, rows(c), :]))
        cum[s, c] = (jnp.dot(tri, hi, preferred_element_type=F32)
                     + jnp.dot(tri, lo, preferred_element_type=F32))

    q_rel, k_rel, q_abs, k_end, decay, val = {}, {}, {}, {}, {}, {}
    for s, c in items:
        b = cum[s, c]
        b_mid = b[blk // 2:blk // 2 + 1, :]
        b_end = b[blk - 1:blk, :]
        qr = q_ref[s, rows(c), :] * jnp.exp(b - b_mid)
        kr = (1.0 - f_ref[s, rows(c), :]) * jnp.exp(b_mid - b)
        q_rel[s, c] = qr.astype(BF16)
        k_rel[s, c] = kr.astype(BF16)
        q_abs[s, c] = (qr * jnp.exp(b_mid)).astype(BF16)
        k_end[s, c] = (kr * jnp.exp(b_end - b_mid)).astype(BF16)
        decay[s, c] = jnp.exp(b_end)
        val[s, c] = i_ref[s, rows(c), :].astype(BF16)

    scores = {}
    for it in items:
        sc = lax.dot_general(q_rel[it], k_rel[it], nt, preferred_element_type=F32)
        scores[it] = jnp.where(causal, sc, 0.0).astype(BF16)

    out, delta = {}, {}
    for it in items:
        out[it] = jnp.dot(scores[it], val[it], preferred_element_type=F32)
        delta[it] = lax.dot_general(val[it], k_end[it], tn, preferred_element_type=F32)

    for s in range(n_streams):
        st = st_ref[s]
        for c in range(n_chunks):
            out[s, c] = out[s, c] + lax.dot_general(q_abs[s, c], st.astype(BF16), nt,
                                                    preferred_element_type=F32)
            st = st * decay[s, c] + delta[s, c]
        st_ref[s] = st

    for s, c in items:
        o = out[s, c]
        y_ref[s, rows(c), :] = (o * _rms_scale(o) * gain * g_ref[s, rows(c), :]).astype(y_ref.dtype)

    @pl.when(tb == pl.num_programs(2) - 1)
    def _():
        for s in range(n_streams):
            s_ref[s] = st_ref[s].T


def hgrn(q, f, i, g, norm_o, s0, *, blk, sb, tb):
    ns, t, width = q.shape
    heads = width // DK_B
    has_s0 = s0 is not None
    tok_spec = pl.BlockSpec((sb, tb, DK_B), lambda b, h, c: (b, c, h))
    state_spec = pl.BlockSpec((sb, None, DK_B, DK_B), lambda b, h, c: (b, h, 0, 0))
    if not has_s0:
        s0 = jnp.zeros((ns, heads, DK_B, DK_B), F32)
    kern = functools.partial(_hgrn_kernel, blk=blk, has_s0=has_s0)
    return pl.pallas_call(
        kern,
        grid=(ns // sb, heads, t // tb),
        in_specs=[tok_spec, tok_spec, tok_spec, tok_spec,
                  pl.BlockSpec((None, 1, DK_B), lambda b, h, c: (h, 0, 0)),
                  state_spec],
        out_specs=[tok_spec, state_spec],
        out_shape=[jax.ShapeDtypeStruct((ns, t, width), BF16),
                   jax.ShapeDtypeStruct((ns, heads, DK_B, DK_B), F32)],
        scratch_shapes=[pltpu.VMEM((sb, DK_B, DK_B), F32)],
        compiler_params=_params("parallel", "parallel", "arbitrary"),
        name="hgrn",
    )(q, f, i, g, norm_o.reshape(heads, 1, DK_B), s0)


def _outproj_kernel(x_ref, ya_ref, yb_ref, wa_ref, wb_ref, o_ref, *wcopy_refs):
    wa, wb = wa_ref[...], wb_ref[...]
    if wcopy_refs:
        wa, wb = wa.astype(BF16), wb.astype(BF16)
        wcopy_refs[0][...] = wa
        wcopy_refs[1][...] = wb
    o_ref[...] = (x_ref[...]
                  + jnp.dot(ya_ref[...], wa, preferred_element_type=F32)
                  + jnp.dot(yb_ref[...], wb, preferred_element_type=F32))


def outproj(x, ya, yb, w_a, w_b, b_block, *, convert, tm, tn):
    m, d = x.shape
    ka = ya.shape[1]
    w_tile = lambda blk: pl.BlockSpec((ka, tn), lambda i, j: (blk, j))
    out_specs = [pl.BlockSpec((tm, tn), lambda i, j: (i, j))]
    out_shape = [jax.ShapeDtypeStruct((m, d), F32)]
    if convert:
        assert m == tm, "weight copies are written once per row block"
        out_specs += [w_tile(0)] * 2
        out_shape += [jax.ShapeDtypeStruct((ka, d), BF16)] * 2
    return pl.pallas_call(
        _outproj_kernel,
        grid=(m // tm, d // tn),
        in_specs=[pl.BlockSpec((tm, tn), lambda i, j: (i, j)),
                  pl.BlockSpec((tm, ka), lambda i, j: (i, 0)),
                  pl.BlockSpec((tm, ka), lambda i, j: (i, 0)),
                  w_tile(0), w_tile(b_block)],
        out_specs=out_specs,
        out_shape=out_shape,
        compiler_params=_params("parallel", "arbitrary"),
        name="outproj_convert" if convert else "outproj",
    )(x, ya, yb, w_a, w_b)


def _mlp_kernel(x_ref, n2_ref, wu_ref, wd_ref, nf_ref, o_ref, *refs, final_norm):
    h_ref = refs[-1]
    f = pl.program_id(1)

    @pl.when(f == 0)
    def _():
        def norm_rows(rs):
            x = x_ref[rs, :]
            h_ref[rs, :] = (x * _rms_scale(x) * n2_ref[...]).astype(h_ref.dtype)
            o_ref[rs, :] = x
        _for_row_chunks(x_ref.shape[0], norm_rows)

    convert = len(refs) > 1
    wu = wu_ref[...]
    if convert:
        wu = wu.astype(BF16)
        refs[0][...] = wu
    hid = jnp.dot(h_ref[...], wu, preferred_element_type=F32)
    hid = jnp.square(jnp.maximum(hid, 0.0)).astype(BF16)
    for n in range(o_ref.shape[1] // MLP_ACC_COLS):
        sl = slice(n * MLP_ACC_COLS, (n + 1) * MLP_ACC_COLS)
        wd = wd_ref[:, sl]
        if convert:
            wd = wd.astype(BF16)
            refs[1][:, sl] = wd
        o_ref[:, sl] += jnp.dot(hid, wd, preferred_element_type=F32)

    if final_norm:
        @pl.when(f == pl.num_programs(1) - 1)
        def _():
            def norm_rows(rs):
                x = o_ref[rs, :]
                o_ref[rs, :] = x * _rms_scale(x) * nf_ref[...]
            _for_row_chunks(o_ref.shape[0], norm_rows)


def mlp(x, norm2, w_up, w_down, norm_f, *, final_norm, convert, tm, tf):
    m, d = x.shape
    ff = w_up.shape[1]
    once = pl.Buffered(1)
    up_tile = pl.BlockSpec((d, tf), lambda i, f: (0, f))
    down_tile = pl.BlockSpec((tf, d), lambda i, f: (f, 0))
    out_specs = [pl.BlockSpec((tm, d), lambda i, f: (i, 0))]
    out_shape = [jax.ShapeDtypeStruct((m, d), F32)]
    if convert:
        assert m == tm, "weight copies are written once per row block"
        out_specs += [up_tile, down_tile]
        out_shape += [jax.ShapeDtypeStruct((d, ff), BF16), jax.ShapeDtypeStruct((ff, d), BF16)]
    kern = functools.partial(_mlp_kernel, final_norm=final_norm)
    return pl.pallas_call(
        kern,
        grid=(m // tm, ff // tf),
        in_specs=[pl.BlockSpec((tm, d), lambda i, f: (i, 0), pipeline_mode=once),
                  pl.BlockSpec((1, d), lambda i, f: (0, 0), pipeline_mode=once),
                  up_tile, down_tile,
                  pl.BlockSpec((1, d), lambda i, f: (0, 0), pipeline_mode=once)],
        out_specs=out_specs,
        out_shape=out_shape,
        scratch_shapes=[pltpu.VMEM((tm, d), BF16)],
        compiler_params=_params("parallel", "arbitrary"),
        name="mlp_convert" if convert else "mlp",
    )(x, norm2.reshape(1, d), w_up, w_down, norm_f.reshape(1, d))


def _trunk_layer(x3, s0, lb_logits, p, *, layer, final_norm, norm_f, emit_v, convert):
    ns, t, d = x3.shape
    m = ns * t
    x = x3.reshape(m, d)
    tm = min(m, 512)
    span = min(t, GMLP_CHUNK)
    small = dict(norm_v=p["norm_v"], lb_logits=lb_logits, w_s=p["w_s"], b_s=p["b_s"])
    bf = {}
    if convert:
        assert m == tm
        outs, bf["w_in"] = inproj_convert(x, p["norm1"], p["w_in"], **small, span=span, layer=layer,
                                          emit_v=emit_v, tk=1024)
    else:
        outs = inproj(x, p["norm1"], p["w_in"], **small, span=span, layer=layer, emit_v=emit_v, tm=tm)
    ya, (q, f, i, g) = outs[0], outs[-4:]
    width = ya.shape[1]
    if t >= 2048:
        sb, tb = 1, 2048
    else:
        sb, tb = ns, t
    per_stream = lambda a: a.reshape(ns, t, width)
    yb3, s_new = hgrn(per_stream(q), per_stream(f), per_stream(i), per_stream(g), p["norm_o"], s0,
                      blk=min(t, CHUNK), sb=sb, tb=tb)
    yb = yb3.reshape(m, width)
    if convert:
        x1, *bf["w_out"] = outproj(x, ya, yb, p["w_out"], p["w_out"], 1, convert=True, tm=tm, tn=512)
        x2, bf["w_up"], bf["w_down"] = mlp(x1, p["norm2"], p["w_up"], p["w_down"], norm_f,
                                           final_norm=final_norm, convert=True, tm=tm, tf=256)
    else:
        x1, = outproj(x, ya, yb, *p["w_out"], 0, convert=False, tm=min(m, 1024), tn=1024)
        x2, = mlp(x1, p["norm2"], p["w_up"], p["w_down"], norm_f, final_norm=final_norm, convert=False,
                  tm=tm, tf=512)
    v_act = per_stream(outs[1]) if emit_v else None
    return x2.reshape(ns, t, d), s_new, v_act, bf


def kernel(x_prompt, x_sample, state_hgrn, norm1, w_in, w_s, b_s, norm_v, lb_logits, norm_o,
           w_out, norm2, w_up, w_down, norm_f):
    depth = w_in.shape[0]
    lb_logits = lb_logits.astype(F32)
    xp, xs = x_prompt, x_sample
    s_prompt, s_sample, v_sample = [], [], []
    for l in range(depth):
        p = dict(norm1=norm1[l], w_in=w_in[l], w_s=w_s[l], b_s=b_s[l], norm_v=norm_v[l],
                 norm_o=norm_o[l], w_out=w_out[l], norm2=norm2[l], w_up=w_up[l], w_down=w_down[l])
        common = dict(layer=l, final_norm=l == depth - 1, norm_f=norm_f)
        xs, ss, vs, bf = _trunk_layer(xs, state_hgrn[l], lb_logits, p, emit_v=True, convert=True, **common)
        xp, sp, _, _ = _trunk_layer(xp, None, lb_logits, {**p, **bf}, emit_v=False, convert=False, **common)
        s_prompt.append(sp)
        s_sample.append(ss)
        v_sample.append(vs)
    return (xp, xs, jnp.stack(s_prompt), jnp.stack(s_sample), jnp.stack(v_sample))
```

```python
import functools

import jax
import jax.numpy as jnp
from jax import lax
from jax.experimental import pallas as pl
from jax.experimental.pallas import tpu as pltpu

EPS = 1e-6
CHUNK = 64
GMLP_CHUNK = 128
HEADS_A = 8
N_GROUPS = 6
DK_B = 128
MLP_ACC_COLS = 1024
NORM_ROWS = 16
BF16 = jnp.bfloat16
F32 = jnp.float32

V7X_VMEM_BYTES = 64 * 1024 * 1024
VMEM_LIMIT_BYTES = V7X_VMEM_BYTES - 4 * 1024 * 1024


def _params(*semantics):
    return pltpu.CompilerParams(dimension_semantics=semantics, vmem_limit_bytes=VMEM_LIMIT_BYTES)


def _rms_scale(x):
    return lax.rsqrt(jnp.mean(x * x, axis=-1, keepdims=True) + EPS)


def _for_row_chunks(n_rows, body):
    for r in range(0, n_rows, NORM_ROWS):
        body(slice(r, r + NORM_ROWS))


def _inproj_epilogue(pre, nv_ref, lb_ref, ws_ref, bs_ref, out_refs, *, span, layer, emit_v):
    if emit_v:
        ya_ref, v_ref, q_ref, f_ref, i_ref, g_ref = out_refs
    else:
        ya_ref, q_ref, f_ref, i_ref, g_ref = out_refs

    u = jax.nn.gelu(pre(0))
    a = jax.nn.gelu(pre(1))
    v = a * _rms_scale(a) * nv_ref[...]
    if emit_v:
        v_ref[...] = v
    row = lax.broadcasted_iota(jnp.int32, (span, span), 0)
    col = lax.broadcasted_iota(jnp.int32, (span, span), 1)
    w = jnp.where(col // CHUNK <= row // CHUNK, ws_ref[:span, :span], 0.0).astype(BF16)
    bias = bs_ref[:span, :]
    for c in range(u.shape[0] // span):
        sl = slice(c * span, (c + 1) * span)
        mixed = jnp.dot(w, v[sl, :].astype(BF16), preferred_element_type=F32) + bias
        ya_ref[sl, :] = (u[sl, :] * mixed).astype(ya_ref.dtype)

    qz = pre(2)
    q_ref[...] = (qz * jax.nn.sigmoid(qz)).astype(q_ref.dtype)

    lg = lb_ref[...]
    e = jnp.exp(lg - jnp.max(lg, axis=0, keepdims=True))
    sm = e / jnp.sum(e, axis=0, keepdims=True)
    lb = jnp.sum(sm[:layer + 1, :], axis=0, keepdims=True)
    f_ref[...] = lb + (1.0 - lb) * jax.nn.sigmoid(pre(3))

    i_ref[...] = pre(4).astype(i_ref.dtype)

    gz = pre(5)
    g_ref[...] = (gz * jax.nn.sigmoid(gz)).astype(g_ref.dtype)


def _inproj_kernel(x_ref, n1_ref, *refs, span, layer, emit_v):
    w_refs = refs[:N_GROUPS]
    nv_ref, lb_ref, ws_ref, bs_ref = refs[N_GROUPS:N_GROUPS + 4]
    out_refs, h_ref = refs[N_GROUPS + 4:-1], refs[-1]

    @pl.when(pl.program_id(1) == 0)
    def _():
        def norm_rows(rs):
            x = x_ref[rs, :]
            h_ref[rs, :] = (x * _rms_scale(x) * n1_ref[...]).astype(h_ref.dtype)
        _for_row_chunks(x_ref.shape[0], norm_rows)

    pre = lambda g: jnp.dot(h_ref[...], w_refs[g][...], preferred_element_type=F32)
    _inproj_epilogue(pre, nv_ref, lb_ref, ws_ref, bs_ref, out_refs, span=span, layer=layer, emit_v=emit_v)


def _inproj_convert_kernel(x_ref, n1_ref, *refs, span, layer, emit_v):
    w_refs = refs[:N_GROUPS]
    nv_ref, lb_ref, ws_ref, bs_ref = refs[N_GROUPS:N_GROUPS + 4]
    n_act = 6 if emit_v else 5
    out_refs = refs[N_GROUPS + 4:N_GROUPS + 4 + n_act]
    wb_refs = refs[N_GROUPS + 4 + n_act:-2]
    h_ref, acc_ref = refs[-2:]
    j, k = pl.program_id(1), pl.program_id(2)
    kt, _, tk = h_ref.shape

    @pl.when(jnp.logical_and(j == 0, k == 0))
    def _():
        def norm_rows(rs):
            x = x_ref[rs, :]
            h = (x * _rms_scale(x) * n1_ref[...]).astype(h_ref.dtype)
            for kk in range(kt):
                h_ref[kk, rs, :] = h[:, kk * tk:(kk + 1) * tk]
        _for_row_chunks(x_ref.shape[0], norm_rows)

    @pl.when(k == 0)
    def _():
        acc_ref[...] = jnp.zeros_like(acc_ref)

    for g in range(N_GROUPS):
        wb = w_refs[g][...].astype(BF16)
        wb_refs[g][...] = wb
        acc_ref[g] += jnp.dot(h_ref[k], wb, preferred_element_type=F32)

    @pl.when(k == kt - 1)
    def _():
        _inproj_epilogue(lambda g: acc_ref[g], nv_ref, lb_ref, ws_ref, bs_ref, out_refs,
                         span=span, layer=layer, emit_v=emit_v)


def inproj(x, norm1, w_groups, norm_v, lb_logits, w_s, b_s, *, span, layer, emit_v, tm):
    m, d = x.shape
    width = w_groups[0].shape[1]
    tn = width // HEADS_A
    tile = pl.BlockSpec((tm, tn), lambda i, j: (i, j))
    out_dtypes = [BF16] + ([F32] if emit_v else []) + [BF16, F32, BF16, BF16]
    kern = functools.partial(_inproj_kernel, span=span, layer=layer, emit_v=emit_v)
    return pl.pallas_call(
        kern,
        grid=(m // tm, width // tn),
        in_specs=[pl.BlockSpec((tm, d), lambda i, j: (i, 0)),
                  pl.BlockSpec((1, d), lambda i, j: (0, 0))]
                 + [pl.BlockSpec((d, tn), lambda i, j: (0, j))] * N_GROUPS
                 + [pl.BlockSpec((1, tn), lambda i, j: (0, j)),
                    pl.BlockSpec((lb_logits.shape[0], tn), lambda i, j: (0, j)),
                    pl.BlockSpec((None, GMLP_CHUNK, GMLP_CHUNK), lambda i, j: (j, 0, 0)),
                    pl.BlockSpec((None, GMLP_CHUNK, 1), lambda i, j: (j, 0, 0))],
        out_specs=[tile] * len(out_dtypes),
        out_shape=[jax.ShapeDtypeStruct((m, width), dt) for dt in out_dtypes],
        scratch_shapes=[pltpu.VMEM((tm, d), BF16)],
        compiler_params=_params("parallel", "arbitrary"),
        name="inproj",
    )(x, norm1.reshape(1, d), *w_groups, norm_v.reshape(1, width), lb_logits,
      w_s, b_s.reshape(HEADS_A, GMLP_CHUNK, 1))


def inproj_convert(x, norm1, w_in, norm_v, lb_logits, w_s, b_s, *, span, layer, emit_v, tk):
    m, d = x.shape
    width = w_in.shape[1] // N_GROUPS
    tn = width // HEADS_A
    nt, kt = width // tn, d // tk
    w_spec = lambda g: pl.BlockSpec((tk, tn), lambda i, j, k: (k, g * nt + j))
    tile = pl.BlockSpec((m, tn), lambda i, j, k: (i, j))
    out_dtypes = [BF16] + ([F32] if emit_v else []) + [BF16, F32, BF16, BF16]
    kern = functools.partial(_inproj_convert_kernel, span=span, layer=layer, emit_v=emit_v)
    outs = pl.pallas_call(
        kern,
        grid=(1, nt, kt),
        in_specs=[pl.BlockSpec((m, d), lambda i, j, k: (i, 0)),
                  pl.BlockSpec((1, d), lambda i, j, k: (0, 0))]
                 + [w_spec(g) for g in range(N_GROUPS)]
                 + [pl.BlockSpec((1, tn), lambda i, j, k: (0, j)),
                    pl.BlockSpec((lb_logits.shape[0], tn), lambda i, j, k: (0, j)),
                    pl.BlockSpec((None, GMLP_CHUNK, GMLP_CHUNK), lambda i, j, k: (j, 0, 0)),
                    pl.BlockSpec((None, GMLP_CHUNK, 1), lambda i, j, k: (j, 0, 0))],
        out_specs=[tile] * len(out_dtypes)
                  + [pl.BlockSpec((tk, tn), lambda i, j, k: (k, j))] * N_GROUPS,
        out_shape=[jax.ShapeDtypeStruct((m, width), dt) for dt in out_dtypes]
                  + [jax.ShapeDtypeStruct((d, width), BF16)] * N_GROUPS,
        scratch_shapes=[pltpu.VMEM((kt, m, tk), BF16), pltpu.VMEM((N_GROUPS, m, tn), F32)],
        compiler_params=_params("arbitrary", "arbitrary", "arbitrary"),
        name="inproj_convert",
    )(x, norm1.reshape(1, d), *([w_in] * N_GROUPS), norm_v.reshape(1, width), lb_logits,
      w_s, b_s.reshape(HEADS_A, GMLP_CHUNK, 1))
    return outs[:len(out_dtypes)], outs[len(out_dtypes):]


def _split2_bf16(x):
    hi = x.astype(BF16)
    lo = (x - hi.astype(F32)).astype(BF16)
    return hi, lo


def _hgrn_kernel(q_ref, f_ref, i_ref, g_ref, no_ref, s0_ref, y_ref, s_ref, st_ref, *, blk, has_s0):
    tb = pl.program_id(2)
    n_streams, tokens, _ = q_ref.shape
    n_chunks = tokens // blk
    items = [(s, c) for s in range(n_streams) for c in range(n_chunks)]
    rows = lambda c: slice(c * blk, (c + 1) * blk)

    @pl.when(tb == 0)
    def _():
        if has_s0:
            for s in range(n_streams):
                st_ref[s] = s0_ref[s].T
        else:
            st_ref[...] = jnp.zeros_like(st_ref)

    row = lax.broadcasted_iota(jnp.int32, (blk, blk), 0)
    col = lax.broadcasted_iota(jnp.int32, (blk, blk), 1)
    causal = col <= row
    tri = causal.astype(BF16)
    gain = no_ref[...]
    nt = (((1,), (1,)), ((), ()))
    tn = (((0,), (0,)), ((), ()))

    cum = {}
    for s, c in items:
        hi, lo = _split2_bf16(jnp.log(f_ref[s, rows(c), :]))
        cum[s, c] = (jnp.dot(tri, hi, preferred_element_type=F32)
                     + jnp.dot(tri, lo, preferred_element_type=F32))

    q_rel, k_rel, q_abs, k_end, decay, val = {}, {}, {}, {}, {}, {}
    for s, c in items:
        b = cum[s, c]
        b_mid = b[blk // 2:blk // 2 + 1, :]
        b_end = b[blk - 1:blk, :]
        qr = q_ref[s, rows(c), :] * jnp.exp(b - b_mid)
        kr = (1.0 - f_ref[s, rows(c), :]) * jnp.exp(b_mid - b)
        q_rel[s, c] = qr.astype(BF16)
        k_rel[s, c] = kr.astype(BF16)
        q_abs[s, c] = (qr * jnp.exp(b_mid)).astype(BF16)
        k_end[s, c] = (kr * jnp.exp(b_end - b_mid)).astype(BF16)
        decay[s, c] = jnp.exp(b_end)
        val[s, c] = i_ref[s, rows(c), :].astype(BF16)

    scores = {}
    for it in items:
        sc = lax.dot_general(q_rel[it], k_rel[it], nt, preferred_element_type=F32)
        scores[it] = jnp.where(causal, sc, 0.0).astype(BF16)

    out, delta = {}, {}
    for it in items:
        out[it] = jnp.dot(scores[it], val[it], preferred_element_type=F32)
        delta[it] = lax.dot_general(val[it], k_end[it], tn, preferred_element_type=F32)

    for s in range(n_streams):
        st = st_ref[s]
        for c in range(n_chunks):
            out[s, c] = out[s, c] + lax.dot_general(q_abs[s, c], st.astype(BF16), nt,
                                                    preferred_element_type=F32)
            st = st * decay[s, c] + delta[s, c]
        st_ref[s] = st

    for s, c in items:
        o = out[s, c]
        y_ref[s, rows(c), :] = (o * _rms_scale(o) * gain * g_ref[s, rows(c), :]).astype(y_ref.dtype)

    @pl.when(tb == pl.num_programs(2) - 1)
    def _():
        for s in range(n_streams):
            s_ref[s] = st_ref[s].T


def hgrn(q, f, i, g, norm_o, s0, *, blk, sb, tb):
    ns, t, width = q.shape
    heads = width // DK_B
    has_s0 = s0 is not None
    tok_spec = pl.BlockSpec((sb, tb, DK_B), lambda b, h, c: (b, c, h))
    state_spec = pl.BlockSpec((sb, None, DK_B, DK_B), lambda b, h, c: (b, h, 0, 0))
    if not has_s0:
        s0 = jnp.zeros((ns, heads, DK_B, DK_B), F32)
    kern = functools.partial(_hgrn_kernel, blk=blk, has_s0=has_s0)
    return pl.pallas_call(
        kern,
        grid=(ns // sb, heads, t // tb),
        in_specs=[tok_spec, tok_spec, tok_spec, tok_spec,
                  pl.BlockSpec((None, 1, DK_B), lambda b, h, c: (h, 0, 0)),
                  state_spec],
        out_specs=[tok_spec, state_spec],
        out_shape=[jax.ShapeDtypeStruct((ns, t, width), BF16),
                   jax.ShapeDtypeStruct((ns, heads, DK_B, DK_B), F32)],
        scratch_shapes=[pltpu.VMEM((sb, DK_B, DK_B), F32)],
        compiler_params=_params("parallel", "parallel", "arbitrary"),
        name="hgrn",
    )(q, f, i, g, norm_o.reshape(heads, 1, DK_B), s0)


def _outproj_kernel(x_ref, ya_ref, yb_ref, wa_ref, wb_ref, o_ref, *wcopy_refs):
    wa, wb = wa_ref[...], wb_ref[...]
    if wcopy_refs:
        wa, wb = wa.astype(BF16), wb.astype(BF16)
        wcopy_refs[0][...] = wa
        wcopy_refs[1][...] = wb
    o_ref[...] = (x_ref[...]
                  + jnp.dot(ya_ref[...], wa, preferred_element_type=F32)
                  + jnp.dot(yb_ref[...], wb, preferred_element_type=F32))


def outproj(x, ya, yb, w_a, w_b, b_block, *, convert, tm, tn):
    m, d = x.shape
    ka = ya.shape[1]
    w_tile = lambda blk: pl.BlockSpec((ka, tn), lambda i, j: (blk, j))
    out_specs = [pl.BlockSpec((tm, tn), lambda i, j: (i, j))]
    out_shape = [jax.ShapeDtypeStruct((m, d), F32)]
    if convert:
        assert m == tm, "weight copies are written once per row block"
        out_specs += [w_tile(0)] * 2
        out_shape += [jax.ShapeDtypeStruct((ka, d), BF16)] * 2
    return pl.pallas_call(
        _outproj_kernel,
        grid=(m // tm, d // tn),
        in_specs=[pl.BlockSpec((tm, tn), lambda i, j: (i, j)),
                  pl.BlockSpec((tm, ka), lambda i, j: (i, 0)),
                  pl.BlockSpec((tm, ka), lambda i, j: (i, 0)),
                  w_tile(0), w_tile(b_block)],
        out_specs=out_specs,
        out_shape=out_shape,
        compiler_params=_params("parallel", "arbitrary"),
        name="outproj_convert" if convert else "outproj",
    )(x, ya, yb, w_a, w_b)


def _mlp_kernel(x_hbm, n2_ref, wu_ref, wd_ref, nf_ref, o_ref, *refs, final_norm):
    h_ref, sem = refs[-2:]
    convert = len(refs) > 2
    i, f = pl.program_id(0), pl.program_id(1)
    tm = o_ref.shape[0]

    @pl.when(f == 0)
    def _():
        load = pltpu.make_async_copy(x_hbm.at[pl.ds(pl.multiple_of(i * tm, tm), tm), :], o_ref, sem)
        load.start()
        load.wait()

        def norm_rows(rs):
            x = o_ref[rs, :]
            h_ref[rs, :] = (x * _rms_scale(x) * n2_ref[...]).astype(h_ref.dtype)
        _for_row_chunks(tm, norm_rows)

    wu = wu_ref[...]
    if convert:
        wu = wu.astype(BF16)
        refs[0][...] = wu
    hid = jnp.dot(h_ref[...], wu, preferred_element_type=F32)
    hid = jnp.square(jnp.maximum(hid, 0.0)).astype(BF16)
    for n in range(o_ref.shape[1] // MLP_ACC_COLS):
        sl = slice(n * MLP_ACC_COLS, (n + 1) * MLP_ACC_COLS)
        wd = wd_ref[:, sl]
        if convert:
            wd = wd.astype(BF16)
            refs[1][:, sl] = wd
        o_ref[:, sl] += jnp.dot(hid, wd, preferred_element_type=F32)

    if final_norm:
        @pl.when(f == pl.num_programs(1) - 1)
        def _():
            def norm_rows(rs):
                x = o_ref[rs, :]
                o_ref[rs, :] = x * _rms_scale(x) * nf_ref[...]
            _for_row_chunks(o_ref.shape[0], norm_rows)


def mlp(x, norm2, w_up, w_down, norm_f, *, final_norm, convert, tm, tf):
    m, d = x.shape
    ff = w_up.shape[1]
    once = pl.Buffered(1)
    up_tile = pl.BlockSpec((d, tf), lambda i, f: (0, f))
    down_tile = pl.BlockSpec((tf, d), lambda i, f: (f, 0))
    out_specs = [pl.BlockSpec((tm, d), lambda i, f: (i, 0))]
    out_shape = [jax.ShapeDtypeStruct((m, d), F32)]
    if convert:
        assert m == tm, "weight copies are written once per row block"
        out_specs += [up_tile, down_tile]
        out_shape += [jax.ShapeDtypeStruct((d, ff), BF16), jax.ShapeDtypeStruct((ff, d), BF16)]
    kern = functools.partial(_mlp_kernel, final_norm=final_norm)
    return pl.pallas_call(
        kern,
        grid=(m // tm, ff // tf),
        in_specs=[pl.BlockSpec(memory_space=pl.ANY),
                  pl.BlockSpec((1, d), lambda i, f: (0, 0), pipeline_mode=once),
                  up_tile, down_tile,
                  pl.BlockSpec((1, d), lambda i, f: (0, 0), pipeline_mode=once)],
        out_specs=out_specs,
        out_shape=out_shape,
        scratch_shapes=[pltpu.VMEM((tm, d), BF16), pltpu.SemaphoreType.DMA(())],
        compiler_params=_params("parallel", "arbitrary"),
        name="mlp_convert" if convert else "mlp",
    )(x, norm2.reshape(1, d), w_up, w_down, norm_f.reshape(1, d))


def _trunk_layer(x3, s0, lb_logits, p, *, layer, final_norm, norm_f, emit_v, convert):
    ns, t, d = x3.shape
    m = ns * t
    x = x3.reshape(m, d)
    tm = min(m, 512)
    span = min(t, GMLP_CHUNK)
    small = dict(norm_v=p["norm_v"], lb_logits=lb_logits, w_s=p["w_s"], b_s=p["b_s"])
    bf = {}
    if convert:
        assert m == tm
        outs, bf["w_in"] = inproj_convert(x, p["norm1"], p["w_in"], **small, span=span, layer=layer,
                                          emit_v=emit_v, tk=1024)
    else:
        outs = inproj(x, p["norm1"], p["w_in"], **small, span=span, layer=layer, emit_v=emit_v, tm=tm)
    ya, (q, f, i, g) = outs[0], outs[-4:]
    width = ya.shape[1]
    if t >= 2048:
        sb, tb = 1, 2048
    else:
        sb, tb = ns, t
    per_stream = lambda a: a.reshape(ns, t, width)
    yb3, s_new = hgrn(per_stream(q), per_stream(f), per_stream(i), per_stream(g), p["norm_o"], s0,
                      blk=min(t, CHUNK), sb=sb, tb=tb)
    yb = yb3.reshape(m, width)
    if convert:
        x1, *bf["w_out"] = outproj(x, ya, yb, p["w_out"], p["w_out"], 1, convert=True, tm=tm, tn=512)
        x2, bf["w_up"], bf["w_down"] = mlp(x1, p["norm2"], p["w_up"], p["w_down"], norm_f,
                                           final_norm=final_norm, convert=True, tm=tm, tf=256)
    else:
        x1, = outproj(x, ya, yb, *p["w_out"], 0, convert=False, tm=min(m, 1024), tn=1024)
        x2, = mlp(x1, p["norm2"], p["w_up"], p["w_down"], norm_f, final_norm=final_norm, convert=False,
                  tm=tm, tf=1024)
    v_act = per_stream(outs[1]) if emit_v else None
    return x2.reshape(ns, t, d), s_new, v_act, bf


def kernel(x_prompt, x_sample, state_hgrn, norm1, w_in, w_s, b_s, norm_v, lb_logits, norm_o,
           w_out, norm2, w_up, w_down, norm_f):
    depth = w_in.shape[0]
    lb_logits = lb_logits.astype(F32)
    xp, xs = x_prompt, x_sample
    s_prompt, s_sample, v_sample = [], [], []
    for l in range(depth):
        p = dict(norm1=norm1[l], w_in=w_in[l], w_s=w_s[l], b_s=b_s[l], norm_v=norm_v[l],
                 norm_o=norm_o[l], w_out=w_out[l], norm2=norm2[l], w_up=w_up[l], w_down=w_down[l])
        common = dict(layer=l, final_norm=l == depth - 1, norm_f=norm_f)
        xs, ss, vs, bf = _trunk_layer(xs, state_hgrn[l], lb_logits, p, emit_v=True, convert=True, **common)
        xp, sp, _, _ = _trunk_layer(xp, None, lb_logits, {**p, **bf}, emit_v=False, convert=False, **common)
        s_prompt.append(sp)
        s_sample.append(ss)
        v_sample.append(vs)
    return (xp, xs, jnp.stack(s_prompt), jnp.stack(s_sample), jnp.stack(v_sample))
```

```python
import functools

import jax
import jax.numpy as jnp
from jax import lax
from jax.experimental import pallas as pl
from jax.experimental.pallas import tpu as pltpu

EPS = 1e-6
CHUNK = 64
GMLP_CHUNK = 128
HEADS_A = 8
N_GROUPS = 6
DK_B = 128
MLP_ACC_COLS = 1024
NORM_ROWS = 16
BF16 = jnp.bfloat16
F32 = jnp.float32

V7X_VMEM_BYTES = 64 * 1024 * 1024
VMEM_LIMIT_BYTES = V7X_VMEM_BYTES - 4 * 1024 * 1024


def _params(*semantics):
    return pltpu.CompilerParams(dimension_semantics=semantics, vmem_limit_bytes=VMEM_LIMIT_BYTES)


def _rms_scale(x):
    return lax.rsqrt(jnp.mean(x * x, axis=-1, keepdims=True) + EPS)


def _for_row_chunks(n_rows, body):
    for r in range(0, n_rows, NORM_ROWS):
        body(slice(r, r + NORM_ROWS))


def _inproj_epilogue(pre, nv_ref, lb_ref, ws_ref, bs_ref, out_refs, *, span, layer, emit_v):
    if emit_v:
        ya_ref, v_ref, q_ref, f_ref, i_ref, g_ref = out_refs
    else:
        ya_ref, q_ref, f_ref, i_ref, g_ref = out_refs

    u = jax.nn.gelu(pre(0))
    a = jax.nn.gelu(pre(1))
    v = a * _rms_scale(a) * nv_ref[...]
    if emit_v:
        v_ref[...] = v
    row = lax.broadcasted_iota(jnp.int32, (span, span), 0)
    col = lax.broadcasted_iota(jnp.int32, (span, span), 1)
    w = jnp.where(col // CHUNK <= row // CHUNK, ws_ref[:span, :span], 0.0).astype(BF16)
    bias = bs_ref[:span, :]
    for c in range(u.shape[0] // span):
        sl = slice(c * span, (c + 1) * span)
        mixed = jnp.dot(w, v[sl, :].astype(BF16), preferred_element_type=F32) + bias
        ya_ref[sl, :] = (u[sl, :] * mixed).astype(ya_ref.dtype)

    qz = pre(2)
    q_ref[...] = (qz * jax.nn.sigmoid(qz)).astype(q_ref.dtype)

    lg = lb_ref[...]
    e = jnp.exp(lg - jnp.max(lg, axis=0, keepdims=True))
    sm = e / jnp.sum(e, axis=0, keepdims=True)
    lb = jnp.sum(sm[:layer + 1, :], axis=0, keepdims=True)
    f_ref[...] = lb + (1.0 - lb) * jax.nn.sigmoid(pre(3))

    i_ref[...] = pre(4).astype(i_ref.dtype)

    gz = pre(5)
    g_ref[...] = (gz * jax.nn.sigmoid(gz)).astype(g_ref.dtype)


def _inproj_kernel(x_ref, n1_ref, *refs, span, layer, emit_v):
    w_refs = refs[:N_GROUPS]
    nv_ref, lb_ref, ws_ref, bs_ref = refs[N_GROUPS:N_GROUPS + 4]
    out_refs, h_ref = refs[N_GROUPS + 4:-1], refs[-1]

    @pl.when(pl.program_id(1) == 0)
    def _():
        def norm_rows(rs):
            x = x_ref[rs, :]
            h_ref[rs, :] = (x * _rms_scale(x) * n1_ref[...]).astype(h_ref.dtype)
        _for_row_chunks(x_ref.shape[0], norm_rows)

    pre = lambda g: jnp.dot(h_ref[...], w_refs[g][...], preferred_element_type=F32)
    _inproj_epilogue(pre, nv_ref, lb_ref, ws_ref, bs_ref, out_refs, span=span, layer=layer, emit_v=emit_v)


def _inproj_convert_kernel(x_ref, n1_ref, *refs, span, layer, emit_v):
    w_refs = refs[:N_GROUPS]
    nv_ref, lb_ref, ws_ref, bs_ref = refs[N_GROUPS:N_GROUPS + 4]
    n_act = 6 if emit_v else 5
    out_refs = refs[N_GROUPS + 4:N_GROUPS + 4 + n_act]
    wb_refs = refs[N_GROUPS + 4 + n_act:-2]
    h_ref, acc_ref = refs[-2:]
    j, k = pl.program_id(1), pl.program_id(2)
    kt, _, tk = h_ref.shape

    @pl.when(jnp.logical_and(j == 0, k == 0))
    def _():
        def norm_rows(rs):
            x = x_ref[rs, :]
            h = (x * _rms_scale(x) * n1_ref[...]).astype(h_ref.dtype)
            for kk in range(kt):
                h_ref[kk, rs, :] = h[:, kk * tk:(kk + 1) * tk]
        _for_row_chunks(x_ref.shape[0], norm_rows)

    @pl.when(k == 0)
    def _():
        acc_ref[...] = jnp.zeros_like(acc_ref)

    for g in range(N_GROUPS):
        wb = w_refs[g][...].astype(BF16)
        wb_refs[g][...] = wb
        acc_ref[g] += jnp.dot(h_ref[k], wb, preferred_element_type=F32)

    @pl.when(k == kt - 1)
    def _():
        _inproj_epilogue(lambda g: acc_ref[g], nv_ref, lb_ref, ws_ref, bs_ref, out_refs,
                         span=span, layer=layer, emit_v=emit_v)


def inproj(x, norm1, w_groups, norm_v, lb_logits, w_s, b_s, *, span, layer, emit_v, tm):
    m, d = x.shape
    width = w_groups[0].shape[1]
    tn = width // HEADS_A
    tile = pl.BlockSpec((tm, tn), lambda i, j: (i, j))
    out_dtypes = [BF16] + ([F32] if emit_v else []) + [BF16, F32, BF16, BF16]
    kern = functools.partial(_inproj_kernel, span=span, layer=layer, emit_v=emit_v)
    return pl.pallas_call(
        kern,
        grid=(m // tm, width // tn),
        in_specs=[pl.BlockSpec((tm, d), lambda i, j: (i, 0)),
                  pl.BlockSpec((1, d), lambda i, j: (0, 0))]
                 + [pl.BlockSpec((d, tn), lambda i, j: (0, j))] * N_GROUPS
                 + [pl.BlockSpec((1, tn), lambda i, j: (0, j)),
                    pl.BlockSpec((lb_logits.shape[0], tn), lambda i, j: (0, j)),
                    pl.BlockSpec((None, GMLP_CHUNK, GMLP_CHUNK), lambda i, j: (j, 0, 0)),
                    pl.BlockSpec((None, GMLP_CHUNK, 1), lambda i, j: (j, 0, 0))],
        out_specs=[tile] * len(out_dtypes),
        out_shape=[jax.ShapeDtypeStruct((m, width), dt) for dt in out_dtypes],
        scratch_shapes=[pltpu.VMEM((tm, d), BF16)],
        compiler_params=_params("parallel", "arbitrary"),
        name="inproj",
    )(x, norm1.reshape(1, d), *w_groups, norm_v.reshape(1, width), lb_logits,
      w_s, b_s.reshape(HEADS_A, GMLP_CHUNK, 1))


def inproj_convert(x, norm1, w_in, norm_v, lb_logits, w_s, b_s, *, span, layer, emit_v, tk):
    m, d = x.shape
    width = w_in.shape[1] // N_GROUPS
    tn = width // HEADS_A
    nt, kt = width // tn, d // tk
    w_spec = lambda g: pl.BlockSpec((tk, tn), lambda i, j, k: (k, g * nt + j))
    tile = pl.BlockSpec((m, tn), lambda i, j, k: (i, j))
    out_dtypes = [BF16] + ([F32] if emit_v else []) + [BF16, F32, BF16, BF16]
    kern = functools.partial(_inproj_convert_kernel, span=span, layer=layer, emit_v=emit_v)
    outs = pl.pallas_call(
        kern,
        grid=(1, nt, kt),
        in_specs=[pl.BlockSpec((m, d), lambda i, j, k: (i, 0)),
                  pl.BlockSpec((1, d), lambda i, j, k: (0, 0))]
                 + [w_spec(g) for g in range(N_GROUPS)]
                 + [pl.BlockSpec((1, tn), lambda i, j, k: (0, j)),
                    pl.BlockSpec((lb_logits.shape[0], tn), lambda i, j, k: (0, j)),
                    pl.BlockSpec((None, GMLP_CHUNK, GMLP_CHUNK), lambda i, j, k: (j, 0, 0)),
                    pl.BlockSpec((None, GMLP_CHUNK, 1), lambda i, j, k: (j, 0, 0))],
        out_specs=[tile] * len(out_dtypes)
                  + [pl.BlockSpec((tk, tn), lambda i, j, k: (k, j))] * N_GROUPS,
        out_shape=[jax.ShapeDtypeStruct((m, width), dt) for dt in out_dtypes]
                  + [jax.ShapeDtypeStruct((d, width), BF16)] * N_GROUPS,
        scratch_shapes=[pltpu.VMEM((kt, m, tk), BF16), pltpu.VMEM((N_GROUPS, m, tn), F32)],
        compiler_params=_params("arbitrary", "arbitrary", "arbitrary"),
        name="inproj_convert",
    )(x, norm1.reshape(1, d), *([w_in] * N_GROUPS), norm_v.reshape(1, width), lb_logits,
      w_s, b_s.reshape(HEADS_A, GMLP_CHUNK, 1))
    return outs[:len(out_dtypes)], outs[len(out_dtypes):]


def _split2_bf16(x):
    hi = x.astype(BF16)
    lo = (x - hi.astype(F32)).astype(BF16)
    return hi, lo


def _hgrn_kernel(q_ref, f_ref, i_ref, g_ref, no_ref, s0_ref, y_ref, s_ref, st_ref, *, blk, has_s0):
    tb = pl.program_id(2)
    n_streams, tokens, _ = q_ref.shape
    n_chunks = tokens // blk
    items = [(s, c) for s in range(n_streams) for c in range(n_chunks)]
    rows = lambda c: slice(c * blk, (c + 1) * blk)

    @pl.when(tb == 0)
    def _():
        if has_s0:
            for s in range(n_streams):
                st_ref[s] = s0_ref[s].T
        else:
            st_ref[...] = jnp.zeros_like(st_ref)

    row = lax.broadcasted_iota(jnp.int32, (blk, blk), 0)
    col = lax.broadcasted_iota(jnp.int32, (blk, blk), 1)
    causal = col <= row
    tri = causal.astype(BF16)
    gain = no_ref[...]
    nt = (((1,), (1,)), ((), ()))
    tn = (((0,), (0,)), ((), ()))

    cum = {}
    for s, c in items:
        hi, lo = _split2_bf16(jnp.log(f_ref[s, rows(c), :]))
        cum[s, c] = (jnp.dot(tri, hi, preferred_element_type=F32)
                     + jnp.dot(tri, lo, preferred_element_type=F32))

    q_rel, k_rel, q_abs, k_end, decay, val = {}, {}, {}, {}, {}, {}
    for s, c in items:
        b = cum[s, c]
        b_mid = b[blk // 2:blk // 2 + 1, :]
        b_end = b[blk - 1:blk, :]
        qr = q_ref[s, rows(c), :] * jnp.exp(b - b_mid)
        kr = (1.0 - f_ref[s, rows(c), :]) * jnp.exp(b_mid - b)
        q_rel[s, c] = qr.astype(BF16)
        k_rel[s, c] = kr.astype(BF16)
        q_abs[s, c] = (qr * jnp.exp(b_mid)).astype(BF16)
        k_end[s, c] = (kr * jnp.exp(b_end - b_mid)).astype(BF16)
        decay[s, c] = jnp.exp(b_end)
        val[s, c] = i_ref[s, rows(c), :].astype(BF16)

    scores = {}
    for it in items:
        sc = lax.dot_general(q_rel[it], k_rel[it], nt, preferred_element_type=F32)
        scores[it] = jnp.where(causal, sc, 0.0).astype(BF16)

    out, delta = {}, {}
    for it in items:
        out[it] = jnp.dot(scores[it], val[it], preferred_element_type=F32)
        delta[it] = lax.dot_general(val[it], k_end[it], tn, preferred_element_type=F32)

    for s in range(n_streams):
        st = st_ref[s]
        for c in range(n_chunks):
            out[s, c] = out[s, c] + lax.dot_general(q_abs[s, c], st.astype(BF16), nt,
                                                    preferred_element_type=F32)
            st = st * decay[s, c] + delta[s, c]
        st_ref[s] = st

    for s, c in items:
        o = out[s, c]
        y_ref[s, rows(c), :] = (o * _rms_scale(o) * gain * g_ref[s, rows(c), :]).astype(y_ref.dtype)

    @pl.when(tb == pl.num_programs(2) - 1)
    def _():
        for s in range(n_streams):
            s_ref[s] = st_ref[s].T


def hgrn(q, f, i, g, norm_o, s0, *, blk, sb, tb):
    ns, t, width = q.shape
    heads = width // DK_B
    has_s0 = s0 is not None
    tok_spec = pl.BlockSpec((sb, tb, DK_B), lambda b, h, c: (b, c, h))
    state_spec = pl.BlockSpec((sb, None, DK_B, DK_B), lambda b, h, c: (b, h, 0, 0))
    if not has_s0:
        s0 = jnp.zeros((ns, heads, DK_B, DK_B), F32)
    kern = functools.partial(_hgrn_kernel, blk=blk, has_s0=has_s0)
    return pl.pallas_call(
        kern,
        grid=(ns // sb, heads, t // tb),
        in_specs=[tok_spec, tok_spec, tok_spec, tok_spec,
                  pl.BlockSpec((None, 1, DK_B), lambda b, h, c: (h, 0, 0)),
                  state_spec],
        out_specs=[tok_spec, state_spec],
        out_shape=[jax.ShapeDtypeStruct((ns, t, width), BF16),
                   jax.ShapeDtypeStruct((ns, heads, DK_B, DK_B), F32)],
        scratch_shapes=[pltpu.VMEM((sb, DK_B, DK_B), F32)],
        compiler_params=_params("parallel", "parallel", "arbitrary"),
        name="hgrn",
    )(q, f, i, g, norm_o.reshape(heads, 1, DK_B), s0)


def _outproj_kernel(x_ref, ya_ref, yb_ref, wa_ref, wb_ref, o_ref, *wcopy_refs):
    wa, wb = wa_ref[...], wb_ref[...]
    if wcopy_refs:
        wa, wb = wa.astype(BF16), wb.astype(BF16)
        wcopy_refs[0][...] = wa
        wcopy_refs[1][...] = wb
    o_ref[...] = (x_ref[...]
                  + jnp.dot(ya_ref[...], wa, preferred_element_type=F32)
                  + jnp.dot(yb_ref[...], wb, preferred_element_type=F32))


def outproj(x, ya, yb, w_a, w_b, b_block, *, convert, tm, tn):
    m, d = x.shape
    ka = ya.shape[1]
    w_tile = lambda blk: pl.BlockSpec((ka, tn), lambda i, j: (blk, j))
    out_specs = [pl.BlockSpec((tm, tn), lambda i, j: (i, j))]
    out_shape = [jax.ShapeDtypeStruct((m, d), F32)]
    if convert:
        assert m == tm, "weight copies are written once per row block"
        out_specs += [w_tile(0)] * 2
        out_shape += [jax.ShapeDtypeStruct((ka, d), BF16)] * 2
    return pl.pallas_call(
        _outproj_kernel,
        grid=(m // tm, d // tn),
        in_specs=[pl.BlockSpec((tm, tn), lambda i, j: (i, j)),
                  pl.BlockSpec((tm, ka), lambda i, j: (i, 0)),
                  pl.BlockSpec((tm, ka), lambda i, j: (i, 0)),
                  w_tile(0), w_tile(b_block)],
        out_specs=out_specs,
        out_shape=out_shape,
        compiler_params=_params("parallel", "arbitrary"),
        name="outproj_convert" if convert else "outproj",
    )(x, ya, yb, w_a, w_b)


def _mlp_kernel(x_hbm, n2_ref, wu_ref, wd_ref, nf_ref, o_ref, *refs, final_norm):
    h_ref, sem = refs[-2:]
    convert = len(refs) > 2
    i, f = pl.program_id(0), pl.program_id(1)
    tm = o_ref.shape[0]

    @pl.when(f == 0)
    def _():
        load = pltpu.make_async_copy(x_hbm.at[pl.ds(pl.multiple_of(i * tm, tm), tm), :], o_ref, sem)
        load.start()
        load.wait()

        def norm_rows(rs):
            x = o_ref[rs, :]
            h_ref[rs, :] = (x * _rms_scale(x) * n2_ref[...]).astype(h_ref.dtype)
        _for_row_chunks(tm, norm_rows)

    wu = wu_ref[...]
    if convert:
        wu = wu.astype(BF16)
        refs[0][...] = wu
    hid = jnp.dot(h_ref[...], wu, preferred_element_type=F32)
    hid = jnp.square(jnp.maximum(hid, 0.0)).astype(BF16)
    for n in range(o_ref.shape[1] // MLP_ACC_COLS):
        sl = slice(n * MLP_ACC_COLS, (n + 1) * MLP_ACC_COLS)
        wd = wd_ref[:, sl]
        if convert:
            wd = wd.astype(BF16)
            refs[1][:, sl] = wd
        o_ref[:, sl] += jnp.dot(hid, wd, preferred_element_type=F32)

    if final_norm:
        @pl.when(f == pl.num_programs(1) - 1)
        def _():
            def norm_rows(rs):
                x = o_ref[rs, :]
                o_ref[rs, :] = x * _rms_scale(x) * nf_ref[...]
            _for_row_chunks(o_ref.shape[0], norm_rows)


def mlp(x, norm2, w_up, w_down, norm_f, *, final_norm, convert, tm, tf):
    m, d = x.shape
    ff = w_up.shape[1]
    once = pl.Buffered(1)
    up_tile = pl.BlockSpec((d, tf), lambda i, f: (0, f))
    down_tile = pl.BlockSpec((tf, d), lambda i, f: (f, 0))
    out_specs = [pl.BlockSpec((tm, d), lambda i, f: (i, 0), pipeline_mode=once)]
    out_shape = [jax.ShapeDtypeStruct((m, d), F32)]
    if convert:
        assert m == tm, "weight copies are written once per row block"
        out_specs += [up_tile, down_tile]
        out_shape += [jax.ShapeDtypeStruct((d, ff), BF16), jax.ShapeDtypeStruct((ff, d), BF16)]
    kern = functools.partial(_mlp_kernel, final_norm=final_norm)
    return pl.pallas_call(
        kern,
        grid=(m // tm, ff // tf),
        in_specs=[pl.BlockSpec(memory_space=pl.ANY),
                  pl.BlockSpec((1, d), lambda i, f: (0, 0), pipeline_mode=once),
                  up_tile, down_tile,
                  pl.BlockSpec((1, d), lambda i, f: (0, 0), pipeline_mode=once)],
        out_specs=out_specs,
        out_shape=out_shape,
        scratch_shapes=[pltpu.VMEM((tm, d), BF16), pltpu.SemaphoreType.DMA(())],
        compiler_params=_params("parallel", "arbitrary"),
        name="mlp_convert" if convert else "mlp",
    )(x, norm2.reshape(1, d), w_up, w_down, norm_f.reshape(1, d))


def _trunk_layer(x3, s0, lb_logits, p, *, layer, final_norm, norm_f, emit_v, convert):
    ns, t, d = x3.shape
    m = ns * t
    x = x3.reshape(m, d)
    tm = min(m, 512)
    span = min(t, GMLP_CHUNK)
    small = dict(norm_v=p["norm_v"], lb_logits=lb_logits, w_s=p["w_s"], b_s=p["b_s"])
    bf = {}
    if convert:
        assert m == tm
        outs, bf["w_in"] = inproj_convert(x, p["norm1"], p["w_in"], **small, span=span, layer=layer,
                                          emit_v=emit_v, tk=1024)
    else:
        outs = inproj(x, p["norm1"], p["w_in"], **small, span=span, layer=layer, emit_v=emit_v, tm=tm)
    ya, (q, f, i, g) = outs[0], outs[-4:]
    width = ya.shape[1]
    if t >= 2048:
        sb, tb = 1, 2048
    else:
        sb, tb = ns, t
    per_stream = lambda a: a.reshape(ns, t, width)
    yb3, s_new = hgrn(per_stream(q), per_stream(f), per_stream(i), per_stream(g), p["norm_o"], s0,
                      blk=min(t, CHUNK), sb=sb, tb=tb)
    yb = yb3.reshape(m, width)
    if convert:
        x1, *bf["w_out"] = outproj(x, ya, yb, p["w_out"], p["w_out"], 1, convert=True, tm=tm, tn=512)
        x2, bf["w_up"], bf["w_down"] = mlp(x1, p["norm2"], p["w_up"], p["w_down"], norm_f,
                                           final_norm=final_norm, convert=True, tm=tm, tf=256)
    else:
        x1, = outproj(x, ya, yb, *p["w_out"], 0, convert=False, tm=min(m, 1024), tn=1024)
        x2, = mlp(x1, p["norm2"], p["w_up"], p["w_down"], norm_f, final_norm=final_norm, convert=False,
                  tm=min(m, 1024), tf=512)
    v_act = per_stream(outs[1]) if emit_v else None
    return x2.reshape(ns, t, d), s_new, v_act, bf


def kernel(x_prompt, x_sample, state_hgrn, norm1, w_in, w_s, b_s, norm_v, lb_logits, norm_o,
           w_out, norm2, w_up, w_down, norm_f):
    depth = w_in.shape[0]
    lb_logits = lb_logits.astype(F32)
    xp, xs = x_prompt, x_sample
    s_prompt, s_sample, v_sample = [], [], []
    for l in range(depth):
        p = dict(norm1=norm1[l], w_in=w_in[l], w_s=w_s[l], b_s=b_s[l], norm_v=norm_v[l],
                 norm_o=norm_o[l], w_out=w_out[l], norm2=norm2[l], w_up=w_up[l], w_down=w_down[l])
        common = dict(layer=l, final_norm=l == depth - 1, norm_f=norm_f)
        xs, ss, vs, bf = _trunk_layer(xs, state_hgrn[l], lb_logits, p, emit_v=True, convert=True, **common)
        xp, sp, _, _ = _trunk_layer(xp, None, lb_logits, {**p, **bf}, emit_v=False, convert=False, **common)
        s_prompt.append(sp)
        s_sample.append(ss)
        v_sample.append(vs)
    return (xp, xs, jnp.stack(s_prompt), jnp.stack(s_sample), jnp.stack(v_sample))
```

```python
import functools

import jax
import jax.numpy as jnp
from jax import lax
from jax.experimental import pallas as pl
from jax.experimental.pallas import tpu as pltpu

EPS = 1e-6
CHUNK = 64
GMLP_CHUNK = 128
HEADS_A = 8
N_GROUPS = 6
DK_B = 128
MLP_ACC_COLS = 1024
NORM_ROWS = 16
BF16 = jnp.bfloat16
F32 = jnp.float32

V7X_VMEM_BYTES = 64 * 1024 * 1024
VMEM_LIMIT_BYTES = V7X_VMEM_BYTES - 4 * 1024 * 1024


def _params(*semantics):
    return pltpu.CompilerParams(dimension_semantics=semantics, vmem_limit_bytes=VMEM_LIMIT_BYTES)


def _rms_scale(x):
    return lax.rsqrt(jnp.mean(x * x, axis=-1, keepdims=True) + EPS)


def _for_row_chunks(n_rows, body):
    for r in range(0, n_rows, NORM_ROWS):
        body(slice(r, r + NORM_ROWS))


def _inproj_epilogue(pre, nv_ref, lb_ref, ws_ref, bs_ref, out_refs, *, span, layer, emit_v):
    if emit_v:
        ya_ref, v_ref, q_ref, f_ref, i_ref, g_ref = out_refs
    else:
        ya_ref, q_ref, f_ref, i_ref, g_ref = out_refs

    u = jax.nn.gelu(pre(0))
    a = jax.nn.gelu(pre(1))
    v = a * _rms_scale(a) * nv_ref[...]
    if emit_v:
        v_ref[...] = v
    row = lax.broadcasted_iota(jnp.int32, (span, span), 0)
    col = lax.broadcasted_iota(jnp.int32, (span, span), 1)
    w = jnp.where(col // CHUNK <= row // CHUNK, ws_ref[:span, :span], 0.0).astype(BF16)
    bias = bs_ref[:span, :]
    for c in range(u.shape[0] // span):
        sl = slice(c * span, (c + 1) * span)
        mixed = jnp.dot(w, v[sl, :].astype(BF16), preferred_element_type=F32) + bias
        ya_ref[sl, :] = (u[sl, :] * mixed).astype(ya_ref.dtype)

    qz = pre(2)
    q_ref[...] = (qz * jax.nn.sigmoid(qz)).astype(q_ref.dtype)

    lg = lb_ref[...]
    e = jnp.exp(lg - jnp.max(lg, axis=0, keepdims=True))
    sm = e / jnp.sum(e, axis=0, keepdims=True)
    lb = jnp.sum(sm[:layer + 1, :], axis=0, keepdims=True)
    f_ref[...] = lb + (1.0 - lb) * jax.nn.sigmoid(pre(3))

    i_ref[...] = pre(4).astype(i_ref.dtype)

    gz = pre(5)
    g_ref[...] = (gz * jax.nn.sigmoid(gz)).astype(g_ref.dtype)


def _inproj_kernel(x_hbm, n1_ref, *refs, span, layer, emit_v):
    w_refs = refs[:N_GROUPS]
    nv_ref, lb_ref, ws_ref, bs_ref = refs[N_GROUPS:N_GROUPS + 4]
    out_refs = refs[N_GROUPS + 4:-3]
    h_ref, x_buf, sem = refs[-3:]
    i, j = pl.program_id(0), pl.program_id(1)
    tm = h_ref.shape[0]

    def x_rows_copy(block):
        rows = pl.ds(pl.multiple_of(block * tm, tm), tm)
        return pltpu.make_async_copy(x_hbm.at[rows, :], x_buf, sem)

    @pl.when(j == 0)
    def _():
        @pl.when(i == 0)
        def _():
            x_rows_copy(0).start()
        x_rows_copy(i).wait()

        def norm_rows(rs):
            x = x_buf[rs, :]
            h_ref[rs, :] = (x * _rms_scale(x) * n1_ref[...]).astype(h_ref.dtype)
        _for_row_chunks(tm, norm_rows)

    @pl.when(jnp.logical_and(j == 1, i + 1 < pl.num_programs(0)))
    def _():
        x_rows_copy(i + 1).start()

    pre = lambda g: jnp.dot(h_ref[...], w_refs[g][...], preferred_element_type=F32)
    _inproj_epilogue(pre, nv_ref, lb_ref, ws_ref, bs_ref, out_refs, span=span, layer=layer, emit_v=emit_v)


def _inproj_convert_kernel(x_ref, n1_ref, *refs, span, layer, emit_v):
    w_refs = refs[:N_GROUPS]
    nv_ref, lb_ref, ws_ref, bs_ref = refs[N_GROUPS:N_GROUPS + 4]
    n_act = 6 if emit_v else 5
    out_refs = refs[N_GROUPS + 4:N_GROUPS + 4 + n_act]
    wb_refs = refs[N_GROUPS + 4 + n_act:-2]
    h_ref, acc_ref = refs[-2:]
    j, k = pl.program_id(1), pl.program_id(2)
    kt, _, tk = h_ref.shape

    @pl.when(jnp.logical_and(j == 0, k == 0))
    def _():
        def norm_rows(rs):
            x = x_ref[rs, :]
            h = (x * _rms_scale(x) * n1_ref[...]).astype(h_ref.dtype)
            for kk in range(kt):
                h_ref[kk, rs, :] = h[:, kk * tk:(kk + 1) * tk]
        _for_row_chunks(x_ref.shape[0], norm_rows)

    @pl.when(k == 0)
    def _():
        acc_ref[...] = jnp.zeros_like(acc_ref)

    for g in range(N_GROUPS):
        wb = w_refs[g][...].astype(BF16)
        wb_refs[g][...] = wb
        acc_ref[g] += jnp.dot(h_ref[k], wb, preferred_element_type=F32)

    @pl.when(k == kt - 1)
    def _():
        _inproj_epilogue(lambda g: acc_ref[g], nv_ref, lb_ref, ws_ref, bs_ref, out_refs,
                         span=span, layer=layer, emit_v=emit_v)


def inproj(x, norm1, w_groups, norm_v, lb_logits, w_s, b_s, *, span, layer, emit_v, tm):
    m, d = x.shape
    width = w_groups[0].shape[1]
    tn = width // HEADS_A
    tile = pl.BlockSpec((tm, tn), lambda i, j: (i, j))
    out_dtypes = [BF16] + ([F32] if emit_v else []) + [BF16, F32, BF16, BF16]
    kern = functools.partial(_inproj_kernel, span=span, layer=layer, emit_v=emit_v)
    return pl.pallas_call(
        kern,
        grid=(m // tm, width // tn),
        in_specs=[pl.BlockSpec(memory_space=pl.ANY),
                  pl.BlockSpec((1, d), lambda i, j: (0, 0))]
                 + [pl.BlockSpec((d, tn), lambda i, j: (0, j))] * N_GROUPS
                 + [pl.BlockSpec((1, tn), lambda i, j: (0, j)),
                    pl.BlockSpec((lb_logits.shape[0], tn), lambda i, j: (0, j)),
                    pl.BlockSpec((None, GMLP_CHUNK, GMLP_CHUNK), lambda i, j: (j, 0, 0)),
                    pl.BlockSpec((None, GMLP_CHUNK, 1), lambda i, j: (j, 0, 0))],
        out_specs=[tile] * len(out_dtypes),
        out_shape=[jax.ShapeDtypeStruct((m, width), dt) for dt in out_dtypes],
        scratch_shapes=[pltpu.VMEM((tm, d), BF16), pltpu.VMEM((tm, d), F32), pltpu.SemaphoreType.DMA(())],
        compiler_params=_params("arbitrary", "arbitrary"),
        name="inproj",
    )(x, norm1.reshape(1, d), *w_groups, norm_v.reshape(1, width), lb_logits,
      w_s, b_s.reshape(HEADS_A, GMLP_CHUNK, 1))


def inproj_convert(x, norm1, w_in, norm_v, lb_logits, w_s, b_s, *, span, layer, emit_v, tk):
    m, d = x.shape
    width = w_in.shape[1] // N_GROUPS
    tn = width // HEADS_A
    nt, kt = width // tn, d // tk
    w_spec = lambda g: pl.BlockSpec((tk, tn), lambda i, j, k: (k, g * nt + j))
    tile = pl.BlockSpec((m, tn), lambda i, j, k: (i, j))
    out_dtypes = [BF16] + ([F32] if emit_v else []) + [BF16, F32, BF16, BF16]
    kern = functools.partial(_inproj_convert_kernel, span=span, layer=layer, emit_v=emit_v)
    outs = pl.pallas_call(
        kern,
        grid=(1, nt, kt),
        in_specs=[pl.BlockSpec((m, d), lambda i, j, k: (i, 0)),
                  pl.BlockSpec((1, d), lambda i, j, k: (0, 0))]
                 + [w_spec(g) for g in range(N_GROUPS)]
                 + [pl.BlockSpec((1, tn), lambda i, j, k: (0, j)),
                    pl.BlockSpec((lb_logits.shape[0], tn), lambda i, j, k: (0, j)),
                    pl.BlockSpec((None, GMLP_CHUNK, GMLP_CHUNK), lambda i, j, k: (j, 0, 0)),
                    pl.BlockSpec((None, GMLP_CHUNK, 1), lambda i, j, k: (j, 0, 0))],
        out_specs=[tile] * len(out_dtypes)
                  + [pl.BlockSpec((tk, tn), lambda i, j, k: (k, j))] * N_GROUPS,
        out_shape=[jax.ShapeDtypeStruct((m, width), dt) for dt in out_dtypes]
                  + [jax.ShapeDtypeStruct((d, width), BF16)] * N_GROUPS,
        scratch_shapes=[pltpu.VMEM((kt, m, tk), BF16), pltpu.VMEM((N_GROUPS, m, tn), F32)],
        compiler_params=_params("arbitrary", "arbitrary", "arbitrary"),
        name="inproj_convert",
    )(x, norm1.reshape(1, d), *([w_in] * N_GROUPS), norm_v.reshape(1, width), lb_logits,
      w_s, b_s.reshape(HEADS_A, GMLP_CHUNK, 1))
    return outs[:len(out_dtypes)], outs[len(out_dtypes):]


def _split2_bf16(x):
    hi = x.astype(BF16)
    lo = (x - hi.astype(F32)).astype(BF16)
    return hi, lo


def _hgrn_kernel(q_ref, f_ref, i_ref, g_ref, no_ref, s0_ref, y_ref, s_ref, st_ref, *, blk, has_s0):
    tb = pl.program_id(2)
    n_streams, tokens, _ = q_ref.shape
    n_chunks = tokens // blk
    items = [(s, c) for s in range(n_streams) for c in range(n_chunks)]
    rows = lambda c: slice(c * blk, (c + 1) * blk)

    @pl.when(tb == 0)
    def _():
        if has_s0:
            for s in range(n_streams):
                st_ref[s] = s0_ref[s].T
        else:
            st_ref[...] = jnp.zeros_like(st_ref)

    row = lax.broadcasted_iota(jnp.int32, (blk, blk), 0)
    col = lax.broadcasted_iota(jnp.int32, (blk, blk), 1)
    causal = col <= row
    tri = causal.astype(BF16)
    gain = no_ref[...]
    nt = (((1,), (1,)), ((), ()))
    tn = (((0,), (0,)), ((), ()))

    cum = {}
    for s, c in items:
        hi, lo = _split2_bf16(jnp.log(f_ref[s, rows(c), :]))
        cum[s, c] = (jnp.dot(tri, hi, preferred_element_type=F32)
                     + jnp.dot(tri, lo, preferred_element_type=F32))

    q_rel, k_rel, q_abs, k_end, decay, val = {}, {}, {}, {}, {}, {}
    for s, c in items:
        b = cum[s, c]
        b_mid = b[blk // 2:blk // 2 + 1, :]
        b_end = b[blk - 1:blk, :]
        qr = q_ref[s, rows(c), :] * jnp.exp(b - b_mid)
        kr = (1.0 - f_ref[s, rows(c), :]) * jnp.exp(b_mid - b)
        q_rel[s, c] = qr.astype(BF16)
        k_rel[s, c] = kr.astype(BF16)
        q_abs[s, c] = (qr * jnp.exp(b_mid)).astype(BF16)
        k_end[s, c] = (kr * jnp.exp(b_end - b_mid)).astype(BF16)
        decay[s, c] = jnp.exp(b_end)
        val[s, c] = i_ref[s, rows(c), :].astype(BF16)

    scores = {}
    for it in items:
        sc = lax.dot_general(q_rel[it], k_rel[it], nt, preferred_element_type=F32)
        scores[it] = jnp.where(causal, sc, 0.0).astype(BF16)

    out, delta = {}, {}
    for it in items:
        out[it] = jnp.dot(scores[it], val[it], preferred_element_type=F32)
        delta[it] = lax.dot_general(val[it], k_end[it], tn, preferred_element_type=F32)

    for s in range(n_streams):
        st = st_ref[s]
        for c in range(n_chunks):
            out[s, c] = out[s, c] + lax.dot_general(q_abs[s, c], st.astype(BF16), nt,
                                                    preferred_element_type=F32)
            st = st * decay[s, c] + delta[s, c]
        st_ref[s] = st

    for s, c in items:
        o = out[s, c]
        y_ref[s, rows(c), :] = (o * _rms_scale(o) * gain * g_ref[s, rows(c), :]).astype(y_ref.dtype)

    @pl.when(tb == pl.num_programs(2) - 1)
    def _():
        for s in range(n_streams):
            s_ref[s] = st_ref[s].T


def hgrn(q, f, i, g, norm_o, s0, *, blk, sb, tb):
    ns, t, width = q.shape
    heads = width // DK_B
    has_s0 = s0 is not None
    tok_spec = pl.BlockSpec((sb, tb, DK_B), lambda b, h, c: (b, c, h))
    state_spec = pl.BlockSpec((sb, None, DK_B, DK_B), lambda b, h, c: (b, h, 0, 0))
    if not has_s0:
        s0 = jnp.zeros((ns, heads, DK_B, DK_B), F32)
    kern = functools.partial(_hgrn_kernel, blk=blk, has_s0=has_s0)
    return pl.pallas_call(
        kern,
        grid=(ns // sb, heads, t // tb),
        in_specs=[tok_spec, tok_spec, tok_spec, tok_spec,
                  pl.BlockSpec((None, 1, DK_B), lambda b, h, c: (h, 0, 0)),
                  state_spec],
        out_specs=[tok_spec, state_spec],
        out_shape=[jax.ShapeDtypeStruct((ns, t, width), BF16),
                   jax.ShapeDtypeStruct((ns, heads, DK_B, DK_B), F32)],
        scratch_shapes=[pltpu.VMEM((sb, DK_B, DK_B), F32)],
        compiler_params=_params("parallel", "parallel", "arbitrary"),
        name="hgrn",
    )(q, f, i, g, norm_o.reshape(heads, 1, DK_B), s0)


def _outproj_kernel(x_ref, ya_ref, yb_ref, wa_ref, wb_ref, o_ref, *wcopy_refs):
    wa, wb = wa_ref[...], wb_ref[...]
    if wcopy_refs:
        wa, wb = wa.astype(BF16), wb.astype(BF16)
        wcopy_refs[0][...] = wa
        wcopy_refs[1][...] = wb
    o_ref[...] = (x_ref[...]
                  + jnp.dot(ya_ref[...], wa, preferred_element_type=F32)
                  + jnp.dot(yb_ref[...], wb, preferred_element_type=F32))


def outproj(x, ya, yb, w_a, w_b, b_block, *, convert, tm, tn):
    m, d = x.shape
    ka = ya.shape[1]
    w_tile = lambda blk: pl.BlockSpec((ka, tn), lambda i, j: (blk, j))
    out_specs = [pl.BlockSpec((tm, tn), lambda i, j: (i, j))]
    out_shape = [jax.ShapeDtypeStruct((m, d), F32)]
    if convert:
        assert m == tm, "weight copies are written once per row block"
        out_specs += [w_tile(0)] * 2
        out_shape += [jax.ShapeDtypeStruct((ka, d), BF16)] * 2
    return pl.pallas_call(
        _outproj_kernel,
        grid=(m // tm, d // tn),
        in_specs=[pl.BlockSpec((tm, tn), lambda i, j: (i, j)),
                  pl.BlockSpec((tm, ka), lambda i, j: (i, 0)),
                  pl.BlockSpec((tm, ka), lambda i, j: (i, 0)),
                  w_tile(0), w_tile(b_block)],
        out_specs=out_specs,
        out_shape=out_shape,
        compiler_params=_params("parallel", "arbitrary"),
        name="outproj_convert" if convert else "outproj",
    )(x, ya, yb, w_a, w_b)


def _mlp_kernel(x_hbm, n2_ref, wu_ref, wd_ref, nf_ref, o_ref, *refs, final_norm):
    h_ref, sem = refs[-2:]
    convert = len(refs) > 2
    i, f = pl.program_id(0), pl.program_id(1)
    tm = o_ref.shape[0]

    @pl.when(f == 0)
    def _():
        load = pltpu.make_async_copy(x_hbm.at[pl.ds(pl.multiple_of(i * tm, tm), tm), :], o_ref, sem)
        load.start()
        load.wait()

        def norm_rows(rs):
            x = o_ref[rs, :]
            h_ref[rs, :] = (x * _rms_scale(x) * n2_ref[...]).astype(h_ref.dtype)
        _for_row_chunks(tm, norm_rows)

    wu = wu_ref[...]
    if convert:
        wu = wu.astype(BF16)
        refs[0][...] = wu
    hid = jnp.dot(h_ref[...], wu, preferred_element_type=F32)
    hid = jnp.square(jnp.maximum(hid, 0.0)).astype(BF16)
    for n in range(o_ref.shape[1] // MLP_ACC_COLS):
        sl = slice(n * MLP_ACC_COLS, (n + 1) * MLP_ACC_COLS)
        wd = wd_ref[:, sl]
        if convert:
            wd = wd.astype(BF16)
            refs[1][:, sl] = wd
        o_ref[:, sl] += jnp.dot(hid, wd, preferred_element_type=F32)

    if final_norm:
        @pl.when(f == pl.num_programs(1) - 1)
        def _():
            def norm_rows(rs):
                x = o_ref[rs, :]
                o_ref[rs, :] = x * _rms_scale(x) * nf_ref[...]
            _for_row_chunks(o_ref.shape[0], norm_rows)


def mlp(x, norm2, w_up, w_down, norm_f, *, final_norm, convert, tm, tf):
    m, d = x.shape
    ff = w_up.shape[1]
    once = pl.Buffered(1)
    up_tile = pl.BlockSpec((d, tf), lambda i, f: (0, f))
    down_tile = pl.BlockSpec((tf, d), lambda i, f: (f, 0))
    out_specs = [pl.BlockSpec((tm, d), lambda i, f: (i, 0))]
    out_shape = [jax.ShapeDtypeStruct((m, d), F32)]
    if convert:
        assert m == tm, "weight copies are written once per row block"
        out_specs += [up_tile, down_tile]
        out_shape += [jax.ShapeDtypeStruct((d, ff), BF16), jax.ShapeDtypeStruct((ff, d), BF16)]
    kern = functools.partial(_mlp_kernel, final_norm=final_norm)
    return pl.pallas_call(
        kern,
        grid=(m // tm, ff // tf),
        in_specs=[pl.BlockSpec(memory_space=pl.ANY),
                  pl.BlockSpec((1, d), lambda i, f: (0, 0), pipeline_mode=once),
                  up_tile, down_tile,
                  pl.BlockSpec((1, d), lambda i, f: (0, 0), pipeline_mode=once)],
        out_specs=out_specs,
        out_shape=out_shape,
        scratch_shapes=[pltpu.VMEM((tm, d), BF16), pltpu.SemaphoreType.DMA(())],
        compiler_params=_params("parallel", "arbitrary"),
        name="mlp_convert" if convert else "mlp",
    )(x, norm2.reshape(1, d), w_up, w_down, norm_f.reshape(1, d))


def _trunk_layer(x3, s0, lb_logits, p, *, layer, final_norm, norm_f, emit_v, convert):
    ns, t, d = x3.shape
    m = ns * t
    x = x3.reshape(m, d)
    tm = min(m, 512)
    span = min(t, GMLP_CHUNK)
    small = dict(norm_v=p["norm_v"], lb_logits=lb_logits, w_s=p["w_s"], b_s=p["b_s"])
    bf = {}
    if convert:
        assert m == tm
        outs, bf["w_in"] = inproj_convert(x, p["norm1"], p["w_in"], **small, span=span, layer=layer,
                                          emit_v=emit_v, tk=1024)
    else:
        outs = inproj(x, p["norm1"], p["w_in"], **small, span=span, layer=layer, emit_v=emit_v,
                      tm=min(m, 1024))
    ya, (q, f, i, g) = outs[0], outs[-4:]
    width = ya.shape[1]
    if t >= 2048:
        sb, tb = 1, 2048
    else:
        sb, tb = ns, t
    per_stream = lambda a: a.reshape(ns, t, width)
    yb3, s_new = hgrn(per_stream(q), per_stream(f), per_stream(i), per_stream(g), p["norm_o"], s0,
                      blk=min(t, CHUNK), sb=sb, tb=tb)
    yb = yb3.reshape(m, width)
    if convert:
        x1, *bf["w_out"] = outproj(x, ya, yb, p["w_out"], p["w_out"], 1, convert=True, tm=tm, tn=512)
        x2, bf["w_up"], bf["w_down"] = mlp(x1, p["norm2"], p["w_up"], p["w_down"], norm_f,
                                           final_norm=final_norm, convert=True, tm=tm, tf=256)
    else:
        x1, = outproj(x, ya, yb, *p["w_out"], 0, convert=False, tm=min(m, 1024), tn=1024)
        x2, = mlp(x1, p["norm2"], p["w_up"], p["w_down"], norm_f, final_norm=final_norm, convert=False,
                  tm=min(m, 1024), tf=512)
    v_act = per_stream(outs[1]) if emit_v else None
    return x2.reshape(ns, t, d), s_new, v_act, bf


def kernel(x_prompt, x_sample, state_hgrn, norm1, w_in, w_s, b_s, norm_v, lb_logits, norm_o,
           w_out, norm2, w_up, w_down, norm_f):
    depth = w_in.shape[0]
    lb_logits = lb_logits.astype(F32)
    xp, xs = x_prompt, x_sample
    s_prompt, s_sample, v_sample = [], [], []
    for l in range(depth):
        p = dict(norm1=norm1[l], w_in=w_in[l], w_s=w_s[l], b_s=b_s[l], norm_v=norm_v[l],
                 norm_o=norm_o[l], w_out=w_out[l], norm2=norm2[l], w_up=w_up[l], w_down=w_down[l])
        common = dict(layer=l, final_norm=l == depth - 1, norm_f=norm_f)
        xs, ss, vs, bf = _trunk_layer(xs, state_hgrn[l], lb_logits, p, emit_v=True, convert=True, **common)
        xp, sp, _, _ = _trunk_layer(xp, None, lb_logits, {**p, **bf}, emit_v=False, convert=False, **common)
        s_prompt.append(sp)
        s_sample.append(ss)
        v_sample.append(vs)
    return (xp, xs, jnp.stack(s_prompt), jnp.stack(s_sample), jnp.stack(v_sample))
```

```python
import functools

import jax
import jax.numpy as jnp
from jax import lax
from jax.experimental import pallas as pl
from jax.experimental.pallas import tpu as pltpu

EPS = 1e-6
CHUNK = 64
GMLP_CHUNK = 128
HEADS_A = 8
N_GROUPS = 6
DK_B = 128
MLP_ACC_COLS = 1024
NORM_ROWS = 16
MLP_WARM_ROWS = 128
BF16 = jnp.bfloat16
F32 = jnp.float32

V7X_VMEM_BYTES = 64 * 1024 * 1024
VMEM_LIMIT_BYTES = V7X_VMEM_BYTES - 4 * 1024 * 1024


def _params(*semantics):
    return pltpu.CompilerParams(dimension_semantics=semantics, vmem_limit_bytes=VMEM_LIMIT_BYTES)


def _rms_scale(x):
    return lax.rsqrt(jnp.mean(x * x, axis=-1, keepdims=True) + EPS)


def _for_row_chunks(n_rows, body):
    for r in range(0, n_rows, NORM_ROWS):
        body(slice(r, r + NORM_ROWS))


def _inproj_epilogue(pre, nv_ref, lb_ref, ws_ref, bs_ref, out_refs, *, span, layer, emit_v):
    if emit_v:
        ya_ref, v_ref, q_ref, f_ref, i_ref, g_ref = out_refs
    else:
        ya_ref, q_ref, f_ref, i_ref, g_ref = out_refs

    u = jax.nn.gelu(pre(0))
    a = jax.nn.gelu(pre(1))
    v = a * _rms_scale(a) * nv_ref[...]
    if emit_v:
        v_ref[...] = v
    row = lax.broadcasted_iota(jnp.int32, (span, span), 0)
    col = lax.broadcasted_iota(jnp.int32, (span, span), 1)
    w = jnp.where(col // CHUNK <= row // CHUNK, ws_ref[:span, :span], 0.0).astype(BF16)
    bias = bs_ref[:span, :]
    for c in range(u.shape[0] // span):
        sl = slice(c * span, (c + 1) * span)
        mixed = jnp.dot(w, v[sl, :].astype(BF16), preferred_element_type=F32) + bias
        ya_ref[sl, :] = (u[sl, :] * mixed).astype(ya_ref.dtype)

    qz = pre(2)
    q_ref[...] = (qz * jax.nn.sigmoid(qz)).astype(q_ref.dtype)

    lg = lb_ref[...]
    e = jnp.exp(lg - jnp.max(lg, axis=0, keepdims=True))
    sm = e / jnp.sum(e, axis=0, keepdims=True)
    lb = jnp.sum(sm[:layer + 1, :], axis=0, keepdims=True)
    f_ref[...] = lb + (1.0 - lb) * jax.nn.sigmoid(pre(3))

    i_ref[...] = pre(4).astype(i_ref.dtype)

    gz = pre(5)
    g_ref[...] = (gz * jax.nn.sigmoid(gz)).astype(g_ref.dtype)


def _inproj_kernel(x_hbm, n1_ref, *refs, span, layer, emit_v):
    w_refs = refs[:N_GROUPS]
    nv_ref, lb_ref, ws_ref, bs_ref = refs[N_GROUPS:N_GROUPS + 4]
    out_refs = refs[N_GROUPS + 4:-3]
    h_ref, x_buf, sem = refs[-3:]
    i, j = pl.program_id(0), pl.program_id(1)
    tm = h_ref.shape[0]

    def x_rows_copy(block):
        rows = pl.ds(pl.multiple_of(block * tm, tm), tm)
        return pltpu.make_async_copy(x_hbm.at[rows, :], x_buf, sem)

    @pl.when(j == 0)
    def _():
        @pl.when(i == 0)
        def _():
            x_rows_copy(0).start()
        x_rows_copy(i).wait()

        def norm_rows(rs):
            x = x_buf[rs, :]
            h_ref[rs, :] = (x * _rms_scale(x) * n1_ref[...]).astype(h_ref.dtype)
        _for_row_chunks(tm, norm_rows)

    @pl.when(jnp.logical_and(j == 1, i + 1 < pl.num_programs(0)))
    def _():
        x_rows_copy(i + 1).start()

    pre = lambda g: jnp.dot(h_ref[...], w_refs[g][...], preferred_element_type=F32)
    _inproj_epilogue(pre, nv_ref, lb_ref, ws_ref, bs_ref, out_refs, span=span, layer=layer, emit_v=emit_v)


def _inproj_convert_kernel(x_ref, n1_ref, *refs, span, layer, emit_v):
    w_refs = refs[:N_GROUPS]
    nv_ref, lb_ref, ws_ref, bs_ref = refs[N_GROUPS:N_GROUPS + 4]
    n_act = 6 if emit_v else 5
    out_refs = refs[N_GROUPS + 4:N_GROUPS + 4 + n_act]
    wb_refs = refs[N_GROUPS + 4 + n_act:-2]
    h_ref, acc_ref = refs[-2:]
    j, k = pl.program_id(1), pl.program_id(2)
    kt, _, tk = h_ref.shape

    @pl.when(jnp.logical_and(j == 0, k == 0))
    def _():
        def norm_rows(rs):
            x = x_ref[rs, :]
            h = (x * _rms_scale(x) * n1_ref[...]).astype(h_ref.dtype)
            for kk in range(kt):
                h_ref[kk, rs, :] = h[:, kk * tk:(kk + 1) * tk]
        _for_row_chunks(x_ref.shape[0], norm_rows)

    @pl.when(k == 0)
    def _():
        acc_ref[...] = jnp.zeros_like(acc_ref)

    for g in range(N_GROUPS):
        wb = w_refs[g][...].astype(BF16)
        wb_refs[g][...] = wb
        acc_ref[g] += jnp.dot(h_ref[k], wb, preferred_element_type=F32)

    @pl.when(k == kt - 1)
    def _():
        _inproj_epilogue(lambda g: acc_ref[g], nv_ref, lb_ref, ws_ref, bs_ref, out_refs,
                         span=span, layer=layer, emit_v=emit_v)


def inproj(x, norm1, w_groups, norm_v, lb_logits, w_s, b_s, *, span, layer, emit_v, tm):
    m, d = x.shape
    width = w_groups[0].shape[1]
    tn = width // HEADS_A
    tile = pl.BlockSpec((tm, tn), lambda i, j: (i, j))
    out_dtypes = [BF16] + ([F32] if emit_v else []) + [BF16, F32, BF16, BF16]
    kern = functools.partial(_inproj_kernel, span=span, layer=layer, emit_v=emit_v)
    return pl.pallas_call(
        kern,
        grid=(m // tm, width // tn),
        in_specs=[pl.BlockSpec(memory_space=pl.ANY),
                  pl.BlockSpec((1, d), lambda i, j: (0, 0))]
                 + [pl.BlockSpec((d, tn), lambda i, j: (0, j))] * N_GROUPS
                 + [pl.BlockSpec((1, tn), lambda i, j: (0, j)),
                    pl.BlockSpec((lb_logits.shape[0], tn), lambda i, j: (0, j)),
                    pl.BlockSpec((None, GMLP_CHUNK, GMLP_CHUNK), lambda i, j: (j, 0, 0)),
                    pl.BlockSpec((None, GMLP_CHUNK, 1), lambda i, j: (j, 0, 0))],
        out_specs=[tile] * len(out_dtypes),
        out_shape=[jax.ShapeDtypeStruct((m, width), dt) for dt in out_dtypes],
        scratch_shapes=[pltpu.VMEM((tm, d), BF16), pltpu.VMEM((tm, d), F32), pltpu.SemaphoreType.DMA(())],
        compiler_params=_params("arbitrary", "arbitrary"),
        name="inproj",
    )(x, norm1.reshape(1, d), *w_groups, norm_v.reshape(1, width), lb_logits,
      w_s, b_s.reshape(HEADS_A, GMLP_CHUNK, 1))


def inproj_convert(x, norm1, w_in, norm_v, lb_logits, w_s, b_s, *, span, layer, emit_v, tk):
    m, d = x.shape
    width = w_in.shape[1] // N_GROUPS
    tn = width // HEADS_A
    nt, kt = width // tn, d // tk
    w_spec = lambda g: pl.BlockSpec((tk, tn), lambda i, j, k: (k, g * nt + j))
    tile = pl.BlockSpec((m, tn), lambda i, j, k: (i, j))
    out_dtypes = [BF16] + ([F32] if emit_v else []) + [BF16, F32, BF16, BF16]
    kern = functools.partial(_inproj_convert_kernel, span=span, layer=layer, emit_v=emit_v)
    outs = pl.pallas_call(
        kern,
        grid=(1, nt, kt),
        in_specs=[pl.BlockSpec((m, d), lambda i, j, k: (i, 0)),
                  pl.BlockSpec((1, d), lambda i, j, k: (0, 0))]
                 + [w_spec(g) for g in range(N_GROUPS)]
                 + [pl.BlockSpec((1, tn), lambda i, j, k: (0, j)),
                    pl.BlockSpec((lb_logits.shape[0], tn), lambda i, j, k: (0, j)),
                    pl.BlockSpec((None, GMLP_CHUNK, GMLP_CHUNK), lambda i, j, k: (j, 0, 0)),
                    pl.BlockSpec((None, GMLP_CHUNK, 1), lambda i, j, k: (j, 0, 0))],
        out_specs=[tile] * len(out_dtypes)
                  + [pl.BlockSpec((tk, tn), lambda i, j, k: (k, j))] * N_GROUPS,
        out_shape=[jax.ShapeDtypeStruct((m, width), dt) for dt in out_dtypes]
                  + [jax.ShapeDtypeStruct((d, width), BF16)] * N_GROUPS,
        scratch_shapes=[pltpu.VMEM((kt, m, tk), BF16), pltpu.VMEM((N_GROUPS, m, tn), F32)],
        compiler_params=_params("arbitrary", "arbitrary", "arbitrary"),
        name="inproj_convert",
    )(x, norm1.reshape(1, d), *([w_in] * N_GROUPS), norm_v.reshape(1, width), lb_logits,
      w_s, b_s.reshape(HEADS_A, GMLP_CHUNK, 1))
    return outs[:len(out_dtypes)], outs[len(out_dtypes):]


def _split2_bf16(x):
    hi = x.astype(BF16)
    lo = (x - hi.astype(F32)).astype(BF16)
    return hi, lo


def _hgrn_kernel(q_ref, f_ref, i_ref, g_ref, no_ref, s0_ref, y_ref, s_ref, st_ref, *, blk, has_s0):
    tb = pl.program_id(2)
    n_streams, tokens, _ = q_ref.shape
    n_chunks = tokens // blk
    items = [(s, c) for s in range(n_streams) for c in range(n_chunks)]
    rows = lambda c: slice(c * blk, (c + 1) * blk)

    @pl.when(tb == 0)
    def _():
        if has_s0:
            for s in range(n_streams):
                st_ref[s] = s0_ref[s].T
        else:
            st_ref[...] = jnp.zeros_like(st_ref)

    row = lax.broadcasted_iota(jnp.int32, (blk, blk), 0)
    col = lax.broadcasted_iota(jnp.int32, (blk, blk), 1)
    causal = col <= row
    tri = causal.astype(BF16)
    gain = no_ref[...]
    nt = (((1,), (1,)), ((), ()))
    tn = (((0,), (0,)), ((), ()))

    cum = {}
    for s, c in items:
        hi, lo = _split2_bf16(jnp.log(f_ref[s, rows(c), :]))
        cum[s, c] = (jnp.dot(tri, hi, preferred_element_type=F32)
                     + jnp.dot(tri, lo, preferred_element_type=F32))

    q_rel, k_rel, q_abs, k_end, decay, val = {}, {}, {}, {}, {}, {}
    for s, c in items:
        b = cum[s, c]
        b_mid = b[blk // 2:blk // 2 + 1, :]
        b_end = b[blk - 1:blk, :]
        qr = q_ref[s, rows(c), :] * jnp.exp(b - b_mid)
        kr = (1.0 - f_ref[s, rows(c), :]) * jnp.exp(b_mid - b)
        q_rel[s, c] = qr.astype(BF16)
        k_rel[s, c] = kr.astype(BF16)
        q_abs[s, c] = (qr * jnp.exp(b_mid)).astype(BF16)
        k_end[s, c] = (kr * jnp.exp(b_end - b_mid)).astype(BF16)
        decay[s, c] = jnp.exp(b_end)
        val[s, c] = i_ref[s, rows(c), :].astype(BF16)

    scores = {}
    for it in items:
        sc = lax.dot_general(q_rel[it], k_rel[it], nt, preferred_element_type=F32)
        scores[it] = jnp.where(causal, sc, 0.0).astype(BF16)

    out, delta = {}, {}
    for it in items:
        out[it] = jnp.dot(scores[it], val[it], preferred_element_type=F32)
        delta[it] = lax.dot_general(val[it], k_end[it], tn, preferred_element_type=F32)

    for s in range(n_streams):
        st = st_ref[s]
        for c in range(n_chunks):
            out[s, c] = out[s, c] + lax.dot_general(q_abs[s, c], st.astype(BF16), nt,
                                                    preferred_element_type=F32)
            st = st * decay[s, c] + delta[s, c]
        st_ref[s] = st

    for s, c in items:
        o = out[s, c]
        y_ref[s, rows(c), :] = (o * _rms_scale(o) * gain * g_ref[s, rows(c), :]).astype(y_ref.dtype)

    @pl.when(tb == pl.num_programs(2) - 1)
    def _():
        for s in range(n_streams):
            s_ref[s] = st_ref[s].T


def hgrn(q, f, i, g, norm_o, s0, *, blk, sb, tb):
    ns, t, width = q.shape
    heads = width // DK_B
    has_s0 = s0 is not None
    tok_spec = pl.BlockSpec((sb, tb, DK_B), lambda b, h, c: (b, c, h))
    state_spec = pl.BlockSpec((sb, None, DK_B, DK_B), lambda b, h, c: (b, h, 0, 0))
    if not has_s0:
        s0 = jnp.zeros((ns, heads, DK_B, DK_B), F32)
    kern = functools.partial(_hgrn_kernel, blk=blk, has_s0=has_s0)
    return pl.pallas_call(
        kern,
        grid=(ns // sb, heads, t // tb),
        in_specs=[tok_spec, tok_spec, tok_spec, tok_spec,
                  pl.BlockSpec((None, 1, DK_B), lambda b, h, c: (h, 0, 0)),
                  state_spec],
        out_specs=[tok_spec, state_spec],
        out_shape=[jax.ShapeDtypeStruct((ns, t, width), BF16),
                   jax.ShapeDtypeStruct((ns, heads, DK_B, DK_B), F32)],
        scratch_shapes=[pltpu.VMEM((sb, DK_B, DK_B), F32)],
        compiler_params=_params("parallel", "parallel", "arbitrary"),
        name="hgrn",
    )(q, f, i, g, norm_o.reshape(heads, 1, DK_B), s0)


def _outproj_kernel(x_ref, ya_ref, yb_ref, wa_ref, wb_ref, o_ref, *wcopy_refs):
    wa, wb = wa_ref[...], wb_ref[...]
    if wcopy_refs:
        wa, wb = wa.astype(BF16), wb.astype(BF16)
        wcopy_refs[0][...] = wa
        wcopy_refs[1][...] = wb
    o_ref[...] = (x_ref[...]
                  + jnp.dot(ya_ref[...], wa, preferred_element_type=F32)
                  + jnp.dot(yb_ref[...], wb, preferred_element_type=F32))


def outproj(x, ya, yb, w_a, w_b, b_block, *, convert, tm, tn):
    m, d = x.shape
    ka = ya.shape[1]
    w_tile = lambda blk: pl.BlockSpec((ka, tn), lambda i, j: (blk, j))
    out_specs = [pl.BlockSpec((tm, tn), lambda i, j: (i, j))]
    out_shape = [jax.ShapeDtypeStruct((m, d), F32)]
    if convert:
        assert m == tm, "weight copies are written once per row block"
        out_specs += [w_tile(0)] * 2
        out_shape += [jax.ShapeDtypeStruct((ka, d), BF16)] * 2
    return pl.pallas_call(
        _outproj_kernel,
        grid=(m // tm, d // tn),
        in_specs=[pl.BlockSpec((tm, tn), lambda i, j: (i, j)),
                  pl.BlockSpec((tm, ka), lambda i, j: (i, 0)),
                  pl.BlockSpec((tm, ka), lambda i, j: (i, 0)),
                  w_tile(0), w_tile(b_block)],
        out_specs=out_specs,
        out_shape=out_shape,
        compiler_params=_params("parallel", "arbitrary"),
        name="outproj_convert" if convert else "outproj",
    )(x, ya, yb, w_a, w_b)


def _mlp_kernel(x_hbm, n2_ref, wu_ref, wd_ref, nf_ref, o_ref, *refs, final_norm):
    h_ref, sem = refs[-2:]
    convert = len(refs) > 2
    i, f = pl.program_id(0), pl.program_id(1)
    tm = o_ref.shape[0]

    @pl.when(f == 0)
    def _():
        load = pltpu.make_async_copy(x_hbm.at[pl.ds(pl.multiple_of(i * tm, tm), tm), :], o_ref, sem)
        load.start()
        load.wait()

        def norm_rows(rs):
            x = o_ref[rs, :]
            h_ref[rs, :] = (x * _rms_scale(x) * n2_ref[...]).astype(h_ref.dtype)
        _for_row_chunks(tm, norm_rows)

    wu = wu_ref[...]
    if convert:
        wu = wu.astype(BF16)
        refs[0][...] = wu
    hid = jnp.dot(h_ref[...], wu, preferred_element_type=F32)
    hid = jnp.square(jnp.maximum(hid, 0.0)).astype(BF16)
    for n in range(o_ref.shape[1] // MLP_ACC_COLS):
        sl = slice(n * MLP_ACC_COLS, (n + 1) * MLP_ACC_COLS)
        wd = wd_ref[:, sl]
        if convert:
            wd = wd.astype(BF16)
            refs[1][:, sl] = wd
        o_ref[:, sl] += jnp.dot(hid, wd, preferred_element_type=F32)

    if final_norm:
        @pl.when(f == pl.num_programs(1) - 1)
        def _():
            def norm_rows(rs):
                x = o_ref[rs, :]
                o_ref[rs, :] = x * _rms_scale(x) * nf_ref[...]
            _for_row_chunks(o_ref.shape[0], norm_rows)


def mlp(x, norm2, w_up, w_down, norm_f, *, final_norm, convert, tm, tf):
    m, d = x.shape
    ff = w_up.shape[1]
    once = pl.Buffered(1)
    up_tile = pl.BlockSpec((d, tf), lambda i, f: (0, f))
    down_tile = pl.BlockSpec((tf, d), lambda i, f: (f, 0))
    out_specs = [pl.BlockSpec((tm, d), lambda i, f: (i, 0))]
    out_shape = [jax.ShapeDtypeStruct((m, d), F32)]
    if convert:
        assert m == tm, "weight copies are written once per row block"
        out_specs += [up_tile, down_tile]
        out_shape += [jax.ShapeDtypeStruct((d, ff), BF16), jax.ShapeDtypeStruct((ff, d), BF16)]
    kern = functools.partial(_mlp_kernel, final_norm=final_norm)
    return pl.pallas_call(
        kern,
        grid=(m // tm, ff // tf),
        in_specs=[pl.BlockSpec(memory_space=pl.ANY),
                  pl.BlockSpec((1, d), lambda i, f: (0, 0), pipeline_mode=once),
                  up_tile, down_tile,
                  pl.BlockSpec((1, d), lambda i, f: (0, 0), pipeline_mode=once)],
        out_specs=out_specs,
        out_shape=out_shape,
        scratch_shapes=[pltpu.VMEM((tm, d), BF16), pltpu.SemaphoreType.DMA(())],
        compiler_params=_params("parallel", "arbitrary"),
        name="mlp_convert" if convert else "mlp",
    )(x, norm2.reshape(1, d), w_up, w_down, norm_f.reshape(1, d))


def _mlp_stream_kernel(x_hbm, xn_ref, xc_ref, n2_ref, wu_ref, wd_ref, nf_ref, o_ref, h_ref, warm_ref, sem,
                       *, final_norm):
    i, f = pl.program_id(0), pl.program_id(1)
    tm, d = o_ref.shape
    step_rows = xn_ref.shape[0]
    slot = i % 2
    norm = lambda x: (x * _rms_scale(x) * n2_ref[...]).astype(h_ref.dtype)

    @pl.when(jnp.logical_and(i == 0, f == 0))
    def _():
        warm_rows = warm_ref.shape[0]
        for c in range(tm // warm_rows):
            load = pltpu.make_async_copy(x_hbm.at[pl.ds(c * warm_rows, warm_rows), :], warm_ref, sem)
            load.start()
            load.wait()
            for r in range(0, warm_rows, NORM_ROWS):
                h_ref[0, c * warm_rows + r:c * warm_rows + r + NORM_ROWS, :] = norm(warm_ref[r:r + NORM_ROWS, :])

    def step(first):
        rs = pl.ds(pl.multiple_of(f * step_rows, step_rows), step_rows)
        if not first:
            o_ref[rs, :] += xc_ref[...]
        hid = jnp.dot(h_ref[slot], wu_ref[...], preferred_element_type=F32)
        hid = jnp.square(jnp.maximum(hid, 0.0)).astype(BF16)
        for n in range(d // MLP_ACC_COLS):
            sl = slice(n * MLP_ACC_COLS, (n + 1) * MLP_ACC_COLS)
            part = jnp.dot(hid, wd_ref[:, sl], preferred_element_type=F32)
            if first:
                o_ref[:, sl] = part
            else:
                o_ref[:, sl] += part
        if first:
            o_ref[rs, :] += xc_ref[...]
        h_ref[1 - slot, rs, :] = norm(xn_ref[...])

    pl.when(f == 0)(lambda: step(True))
    pl.when(f > 0)(lambda: step(False))

    if final_norm:
        @pl.when(f == pl.num_programs(1) - 1)
        def _():
            def norm_rows(rs):
                x = o_ref[rs, :]
                o_ref[rs, :] = x * _rms_scale(x) * nf_ref[...]
            _for_row_chunks(tm, norm_rows)


def mlp_stream(x, norm2, w_up, w_down, norm_f, *, final_norm, tm, tf):
    m, d = x.shape
    ff = w_up.shape[1]
    n_row_blocks, n_f = m // tm, ff // tf
    step_rows = tm // n_f
    once = pl.Buffered(1)
    kern = functools.partial(_mlp_stream_kernel, final_norm=final_norm)
    return pl.pallas_call(
        kern,
        grid=(n_row_blocks, n_f),
        in_specs=[pl.BlockSpec(memory_space=pl.ANY),
                  pl.BlockSpec((step_rows, d), lambda i, f: (jnp.minimum(i + 1, n_row_blocks - 1) * n_f + f, 0)),
                  pl.BlockSpec((step_rows, d), lambda i, f: (i * n_f + f, 0)),
                  pl.BlockSpec((1, d), lambda i, f: (0, 0), pipeline_mode=once),
                  pl.BlockSpec((d, tf), lambda i, f: (0, f)),
                  pl.BlockSpec((tf, d), lambda i, f: (f, 0)),
                  pl.BlockSpec((1, d), lambda i, f: (0, 0), pipeline_mode=once)],
        out_specs=pl.BlockSpec((tm, d), lambda i, f: (i, 0), pipeline_mode=once),
        out_shape=jax.ShapeDtypeStruct((m, d), F32),
        scratch_shapes=[pltpu.VMEM((2, tm, d), BF16), pltpu.VMEM((MLP_WARM_ROWS, d), F32),
                        pltpu.SemaphoreType.DMA(())],
        compiler_params=_params("arbitrary", "arbitrary"),
        name="mlp",
    )(x, x, x, norm2.reshape(1, d), w_up, w_down, norm_f.reshape(1, d))


def _trunk_layer(x3, s0, lb_logits, p, *, layer, final_norm, norm_f, emit_v, convert):
    ns, t, d = x3.shape
    m = ns * t
    x = x3.reshape(m, d)
    tm = min(m, 512)
    span = min(t, GMLP_CHUNK)
    small = dict(norm_v=p["norm_v"], lb_logits=lb_logits, w_s=p["w_s"], b_s=p["b_s"])
    bf = {}
    if convert:
        assert m == tm
        outs, bf["w_in"] = inproj_convert(x, p["norm1"], p["w_in"], **small, span=span, layer=layer,
                                          emit_v=emit_v, tk=1024)
    else:
        outs = inproj(x, p["norm1"], p["w_in"], **small, span=span, layer=layer, emit_v=emit_v,
                      tm=min(m, 1024))
    ya, (q, f, i, g) = outs[0], outs[-4:]
    width = ya.shape[1]
    if t >= 2048:
        sb, tb = 1, 2048
    else:
        sb, tb = ns, t
    per_stream = lambda a: a.reshape(ns, t, width)
    yb3, s_new = hgrn(per_stream(q), per_stream(f), per_stream(i), per_stream(g), p["norm_o"], s0,
                      blk=min(t, CHUNK), sb=sb, tb=tb)
    yb = yb3.reshape(m, width)
    if convert:
        x1, *bf["w_out"] = outproj(x, ya, yb, p["w_out"], p["w_out"], 1, convert=True, tm=tm, tn=512)
        x2, bf["w_up"], bf["w_down"] = mlp(x1, p["norm2"], p["w_up"], p["w_down"], norm_f,
                                           final_norm=final_norm, convert=True, tm=tm, tf=256)
    else:
        x1, = outproj(x, ya, yb, *p["w_out"], 0, convert=False, tm=min(m, 1024), tn=1024)
        x2 = mlp_stream(x1, p["norm2"], p["w_up"], p["w_down"], norm_f, final_norm=final_norm,
                        tm=min(m, 1024), tf=512)
    v_act = per_stream(outs[1]) if emit_v else None
    return x2.reshape(ns, t, d), s_new, v_act, bf


def kernel(x_prompt, x_sample, state_hgrn, norm1, w_in, w_s, b_s, norm_v, lb_logits, norm_o,
           w_out, norm2, w_up, w_down, norm_f):
    depth = w_in.shape[0]
    lb_logits = lb_logits.astype(F32)
    xp, xs = x_prompt, x_sample
    s_prompt, s_sample, v_sample = [], [], []
    for l in range(depth):
        p = dict(norm1=norm1[l], w_in=w_in[l], w_s=w_s[l], b_s=b_s[l], norm_v=norm_v[l],
                 norm_o=norm_o[l], w_out=w_out[l], norm2=norm2[l], w_up=w_up[l], w_down=w_down[l])
        common = dict(layer=l, final_norm=l == depth - 1, norm_f=norm_f)
        xs, ss, vs, bf = _trunk_layer(xs, state_hgrn[l], lb_logits, p, emit_v=True, convert=True, **common)
        xp, sp, _, _ = _trunk_layer(xp, None, lb_logits, {**p, **bf}, emit_v=False, convert=False, **common)
        s_prompt.append(sp)
        s_sample.append(ss)
        v_sample.append(vs)
    return (xp, xs, jnp.stack(s_prompt), jnp.stack(s_sample), jnp.stack(v_sample))
```

```python
import functools

import jax
import jax.numpy as jnp
from jax import lax
from jax.experimental import pallas as pl
from jax.experimental.pallas import tpu as pltpu

EPS = 1e-6
CHUNK = 64
GMLP_CHUNK = 128
HEADS_A = 8
N_GROUPS = 6
DK_B = 128
MLP_ACC_COLS = 1024
NORM_ROWS = 16
MLP_WARM_ROWS = 128
BF16 = jnp.bfloat16
F32 = jnp.float32

V7X_VMEM_BYTES = 64 * 1024 * 1024
VMEM_LIMIT_BYTES = V7X_VMEM_BYTES - 4 * 1024 * 1024


def _params(*semantics):
    return pltpu.CompilerParams(dimension_semantics=semantics, vmem_limit_bytes=VMEM_LIMIT_BYTES)


def _rms_scale(x):
    return lax.rsqrt(jnp.mean(x * x, axis=-1, keepdims=True) + EPS)


def _for_row_chunks(n_rows, body):
    for r in range(0, n_rows, NORM_ROWS):
        body(slice(r, r + NORM_ROWS))


def _inproj_epilogue(pre, nv_ref, lb_ref, ws_ref, bs_ref, out_refs, *, span, layer, emit_v):
    if emit_v:
        ya_ref, v_ref, q_ref, f_ref, i_ref, g_ref = out_refs
    else:
        ya_ref, q_ref, f_ref, i_ref, g_ref = out_refs

    u = jax.nn.gelu(pre(0))
    a = jax.nn.gelu(pre(1))
    v = a * _rms_scale(a) * nv_ref[...]
    if emit_v:
        v_ref[...] = v
    row = lax.broadcasted_iota(jnp.int32, (span, span), 0)
    col = lax.broadcasted_iota(jnp.int32, (span, span), 1)
    w = jnp.where(col // CHUNK <= row // CHUNK, ws_ref[:span, :span], 0.0).astype(BF16)
    bias = bs_ref[:span, :]
    for c in range(u.shape[0] // span):
        sl = slice(c * span, (c + 1) * span)
        mixed = jnp.dot(w, v[sl, :].astype(BF16), preferred_element_type=F32) + bias
        ya_ref[sl, :] = (u[sl, :] * mixed).astype(ya_ref.dtype)

    qz = pre(2)
    q_ref[...] = (qz * jax.nn.sigmoid(qz)).astype(q_ref.dtype)

    lg = lb_ref[...]
    e = jnp.exp(lg - jnp.max(lg, axis=0, keepdims=True))
    sm = e / jnp.sum(e, axis=0, keepdims=True)
    lb = jnp.sum(sm[:layer + 1, :], axis=0, keepdims=True)
    f_ref[...] = lb + (1.0 - lb) * jax.nn.sigmoid(pre(3))

    i_ref[...] = pre(4).astype(i_ref.dtype)

    gz = pre(5)
    g_ref[...] = (gz * jax.nn.sigmoid(gz)).astype(g_ref.dtype)


def _warm_up_rows(x_hbm, warm_ref, sem, n_rows, store_normed):
    warm_rows = warm_ref.shape[0]
    for c in range(n_rows // warm_rows):
        load = pltpu.make_async_copy(x_hbm.at[pl.ds(c * warm_rows, warm_rows), :], warm_ref, sem)
        load.start()
        load.wait()
        for r in range(0, warm_rows, NORM_ROWS):
            store_normed(slice(c * warm_rows + r, c * warm_rows + r + NORM_ROWS), warm_ref[r:r + NORM_ROWS, :])


def _inproj_kernel(x_hbm, n1_ref, *refs, span, layer, emit_v):
    w_refs = refs[:N_GROUPS]
    nv_ref, lb_ref, ws_ref, bs_ref = refs[N_GROUPS:N_GROUPS + 4]
    out_refs = refs[N_GROUPS + 4:-3]
    h_ref, x_buf, sem = refs[-3:]
    i, j = pl.program_id(0), pl.program_id(1)
    tm = h_ref.shape[0]

    def x_rows_copy(block):
        rows = pl.ds(pl.multiple_of(block * tm, tm), tm)
        return pltpu.make_async_copy(x_hbm.at[rows, :], x_buf, sem)

    @pl.when(j == 0)
    def _():
        @pl.when(i == 0)
        def _():
            x_rows_copy(0).start()
        x_rows_copy(i).wait()

        def norm_rows(rs):
            x = x_buf[rs, :]
            h_ref[rs, :] = (x * _rms_scale(x) * n1_ref[...]).astype(h_ref.dtype)
        _for_row_chunks(tm, norm_rows)

    @pl.when(jnp.logical_and(j == 1, i + 1 < pl.num_programs(0)))
    def _():
        x_rows_copy(i + 1).start()

    pre = lambda g: jnp.dot(h_ref[...], w_refs[g][...], preferred_element_type=F32)
    _inproj_epilogue(pre, nv_ref, lb_ref, ws_ref, bs_ref, out_refs, span=span, layer=layer, emit_v=emit_v)


def _inproj_convert_kernel(x_ref, n1_ref, *refs, span, layer, emit_v):
    w_refs = refs[:N_GROUPS]
    nv_ref, lb_ref, ws_ref, bs_ref = refs[N_GROUPS:N_GROUPS + 4]
    n_act = 6 if emit_v else 5
    out_refs = refs[N_GROUPS + 4:N_GROUPS + 4 + n_act]
    wb_refs = refs[N_GROUPS + 4 + n_act:-2]
    h_ref, acc_ref = refs[-2:]
    j, k = pl.program_id(1), pl.program_id(2)
    kt, _, tk = h_ref.shape

    @pl.when(jnp.logical_and(j == 0, k == 0))
    def _():
        def norm_rows(rs):
            x = x_ref[rs, :]
            h = (x * _rms_scale(x) * n1_ref[...]).astype(h_ref.dtype)
            for kk in range(kt):
                h_ref[kk, rs, :] = h[:, kk * tk:(kk + 1) * tk]
        _for_row_chunks(x_ref.shape[0], norm_rows)

    @pl.when(k == 0)
    def _():
        acc_ref[...] = jnp.zeros_like(acc_ref)

    for g in range(N_GROUPS):
        wb = w_refs[g][...].astype(BF16)
        wb_refs[g][...] = wb
        acc_ref[g] += jnp.dot(h_ref[k], wb, preferred_element_type=F32)

    @pl.when(k == kt - 1)
    def _():
        _inproj_epilogue(lambda g: acc_ref[g], nv_ref, lb_ref, ws_ref, bs_ref, out_refs,
                         span=span, layer=layer, emit_v=emit_v)


def inproj(x, norm1, w_groups, norm_v, lb_logits, w_s, b_s, *, span, layer, emit_v, tm):
    m, d = x.shape
    width = w_groups[0].shape[1]
    tn = width // HEADS_A
    tile = pl.BlockSpec((tm, tn), lambda i, j: (i, j))
    out_dtypes = [BF16] + ([F32] if emit_v else []) + [BF16, F32, BF16, BF16]
    kern = functools.partial(_inproj_kernel, span=span, layer=layer, emit_v=emit_v)
    return pl.pallas_call(
        kern,
        grid=(m // tm, width // tn),
        in_specs=[pl.BlockSpec(memory_space=pl.ANY),
                  pl.BlockSpec((1, d), lambda i, j: (0, 0))]
                 + [pl.BlockSpec((d, tn), lambda i, j: (0, j))] * N_GROUPS
                 + [pl.BlockSpec((1, tn), lambda i, j: (0, j)),
                    pl.BlockSpec((lb_logits.shape[0], tn), lambda i, j: (0, j)),
                    pl.BlockSpec((None, GMLP_CHUNK, GMLP_CHUNK), lambda i, j: (j, 0, 0)),
                    pl.BlockSpec((None, GMLP_CHUNK, 1), lambda i, j: (j, 0, 0))],
        out_specs=[tile] * len(out_dtypes),
        out_shape=[jax.ShapeDtypeStruct((m, width), dt) for dt in out_dtypes],
        scratch_shapes=[pltpu.VMEM((tm, d), BF16), pltpu.VMEM((tm, d), F32), pltpu.SemaphoreType.DMA(())],
        compiler_params=_params("arbitrary", "arbitrary"),
        name="inproj",
    )(x, norm1.reshape(1, d), *w_groups, norm_v.reshape(1, width), lb_logits,
      w_s, b_s.reshape(HEADS_A, GMLP_CHUNK, 1))


def inproj_convert(x, norm1, w_in, norm_v, lb_logits, w_s, b_s, *, span, layer, emit_v, tk):
    m, d = x.shape
    width = w_in.shape[1] // N_GROUPS
    tn = width // HEADS_A
    nt, kt = width // tn, d // tk
    w_spec = lambda g: pl.BlockSpec((tk, tn), lambda i, j, k: (k, g * nt + j))
    tile = pl.BlockSpec((m, tn), lambda i, j, k: (i, j))
    out_dtypes = [BF16] + ([F32] if emit_v else []) + [BF16, F32, BF16, BF16]
    kern = functools.partial(_inproj_convert_kernel, span=span, layer=layer, emit_v=emit_v)
    outs = pl.pallas_call(
        kern,
        grid=(1, nt, kt),
        in_specs=[pl.BlockSpec((m, d), lambda i, j, k: (i, 0)),
                  pl.BlockSpec((1, d), lambda i, j, k: (0, 0))]
                 + [w_spec(g) for g in range(N_GROUPS)]
                 + [pl.BlockSpec((1, tn), lambda i, j, k: (0, j)),
                    pl.BlockSpec((lb_logits.shape[0], tn), lambda i, j, k: (0, j)),
                    pl.BlockSpec((None, GMLP_CHUNK, GMLP_CHUNK), lambda i, j, k: (j, 0, 0)),
                    pl.BlockSpec((None, GMLP_CHUNK, 1), lambda i, j, k: (j, 0, 0))],
        out_specs=[tile] * len(out_dtypes)
                  + [pl.BlockSpec((tk, tn), lambda i, j, k: (k, j))] * N_GROUPS,
        out_shape=[jax.ShapeDtypeStruct((m, width), dt) for dt in out_dtypes]
                  + [jax.ShapeDtypeStruct((d, width), BF16)] * N_GROUPS,
        scratch_shapes=[pltpu.VMEM((kt, m, tk), BF16), pltpu.VMEM((N_GROUPS, m, tn), F32)],
        compiler_params=_params("arbitrary", "arbitrary", "arbitrary"),
        name="inproj_convert",
    )(x, norm1.reshape(1, d), *([w_in] * N_GROUPS), norm_v.reshape(1, width), lb_logits,
      w_s, b_s.reshape(HEADS_A, GMLP_CHUNK, 1))
    return outs[:len(out_dtypes)], outs[len(out_dtypes):]


def _split2_bf16(x):
    hi = x.astype(BF16)
    lo = (x - hi.astype(F32)).astype(BF16)
    return hi, lo


def _hgrn_kernel(q_ref, f_ref, i_ref, g_ref, no_ref, s0_ref, *refs, blk, has_s0):
    n_side = (len(refs) - 3) // 2
    side_in, side_out = refs[:n_side], refs[n_side + 2:-1]
    y_ref, s_ref, st_ref = refs[n_side], refs[n_side + 1], refs[-1]
    tb = pl.program_id(2)
    n_streams, tokens, _ = q_ref.shape
    n_chunks = tokens // blk
    items = [(s, c) for s in range(n_streams) for c in range(n_chunks)]
    rows = lambda c: slice(c * blk, (c + 1) * blk)

    @pl.when(tb == 0)
    def _():
        if has_s0:
            for s in range(n_streams):
                st_ref[s] = s0_ref[s].T
        else:
            st_ref[...] = jnp.zeros_like(st_ref)

    row = lax.broadcasted_iota(jnp.int32, (blk, blk), 0)
    col = lax.broadcasted_iota(jnp.int32, (blk, blk), 1)
    causal = col <= row
    tri = causal.astype(BF16)
    gain = no_ref[...]
    nt = (((1,), (1,)), ((), ()))
    tn = (((0,), (0,)), ((), ()))

    for w_ref, wb_ref in zip(side_in, side_out):
        wb_ref[...] = w_ref[...].astype(wb_ref.dtype)

    cum = {}
    for s, c in items:
        hi, lo = _split2_bf16(jnp.log(f_ref[s, rows(c), :]))
        cum[s, c] = (jnp.dot(tri, hi, preferred_element_type=F32)
                     + jnp.dot(tri, lo, preferred_element_type=F32))

    q_rel, k_rel, q_abs, k_end, decay, val = {}, {}, {}, {}, {}, {}
    for s, c in items:
        b = cum[s, c]
        b_mid = b[blk // 2:blk // 2 + 1, :]
        b_end = b[blk - 1:blk, :]
        qr = q_ref[s, rows(c), :] * jnp.exp(b - b_mid)
        kr = (1.0 - f_ref[s, rows(c), :]) * jnp.exp(b_mid - b)
        q_rel[s, c] = qr.astype(BF16)
        k_rel[s, c] = kr.astype(BF16)
        q_abs[s, c] = (qr * jnp.exp(b_mid)).astype(BF16)
        k_end[s, c] = (kr * jnp.exp(b_end - b_mid)).astype(BF16)
        decay[s, c] = jnp.exp(b_end)
        val[s, c] = i_ref[s, rows(c), :].astype(BF16)

    scores = {}
    for it in items:
        sc = lax.dot_general(q_rel[it], k_rel[it], nt, preferred_element_type=F32)
        scores[it] = jnp.where(causal, sc, 0.0).astype(BF16)

    out, delta = {}, {}
    for it in items:
        out[it] = jnp.dot(scores[it], val[it], preferred_element_type=F32)
        delta[it] = lax.dot_general(val[it], k_end[it], tn, preferred_element_type=F32)

    for s in range(n_streams):
        st = st_ref[s]
        for c in range(n_chunks):
            out[s, c] = out[s, c] + lax.dot_general(q_abs[s, c], st.astype(BF16), nt,
                                                    preferred_element_type=F32)
            st = st * decay[s, c] + delta[s, c]
        st_ref[s] = st

    for s, c in items:
        o = out[s, c]
        y_ref[s, rows(c), :] = (o * _rms_scale(o) * gain * g_ref[s, rows(c), :]).astype(y_ref.dtype)

    @pl.when(tb == pl.num_programs(2) - 1)
    def _():
        for s in range(n_streams):
            s_ref[s] = st_ref[s].T


def hgrn(q, f, i, g, norm_o, s0, *, blk, sb, tb, side_weights=()):
    ns, t, width = q.shape
    heads = width // DK_B
    has_s0 = s0 is not None
    grid = (ns // sb, heads, t // tb)
    n_steps = grid[0] * grid[1] * grid[2]
    tok_spec = pl.BlockSpec((sb, tb, DK_B), lambda b, h, c: (b, c, h))
    state_spec = pl.BlockSpec((sb, None, DK_B, DK_B), lambda b, h, c: (b, h, 0, 0))
    side_specs = []
    for w in side_weights:
        assert w.shape[0] % (n_steps * NORM_ROWS) == 0, "side windows must be whole packed bf16 row tiles"
        side_specs.append(pl.BlockSpec((w.shape[0] // n_steps, w.shape[1]),
                                       lambda b, h, c: ((b * grid[1] + h) * grid[2] + c, 0)))
    if not has_s0:
        s0 = jnp.zeros((ns, heads, DK_B, DK_B), F32)
    kern = functools.partial(_hgrn_kernel, blk=blk, has_s0=has_s0)
    return pl.pallas_call(
        kern,
        grid=grid,
        in_specs=[tok_spec, tok_spec, tok_spec, tok_spec,
                  pl.BlockSpec((None, 1, DK_B), lambda b, h, c: (h, 0, 0)),
                  state_spec] + side_specs,
        out_specs=[tok_spec, state_spec] + side_specs,
        out_shape=[jax.ShapeDtypeStruct((ns, t, width), BF16),
                   jax.ShapeDtypeStruct((ns, heads, DK_B, DK_B), F32)]
                  + [jax.ShapeDtypeStruct(w.shape, BF16) for w in side_weights],
        scratch_shapes=[pltpu.VMEM((sb, DK_B, DK_B), F32)],
        compiler_params=_params("parallel", "parallel", "arbitrary"),
        name="hgrn",
    )(q, f, i, g, norm_o.reshape(heads, 1, DK_B), s0, *side_weights)


def _outproj_kernel(x_ref, ya_ref, yb_ref, wa_ref, wb_ref, o_ref, *wcopy_refs):
    wa, wb = wa_ref[...], wb_ref[...]
    if wcopy_refs:
        wa, wb = wa.astype(BF16), wb.astype(BF16)
        wcopy_refs[0][...] = wa
        wcopy_refs[1][...] = wb
    o_ref[...] = (x_ref[...]
                  + jnp.dot(ya_ref[...], wa, preferred_element_type=F32)
                  + jnp.dot(yb_ref[...], wb, preferred_element_type=F32))


def outproj(x, ya, yb, w_a, w_b, b_block, *, convert, tm, tn):
    m, d = x.shape
    ka = ya.shape[1]
    w_tile = lambda blk: pl.BlockSpec((ka, tn), lambda i, j: (blk, j))
    out_specs = [pl.BlockSpec((tm, tn), lambda i, j: (i, j))]
    out_shape = [jax.ShapeDtypeStruct((m, d), F32)]
    if convert:
        assert m == tm, "weight copies are written once per row block"
        out_specs += [w_tile(0)] * 2
        out_shape += [jax.ShapeDtypeStruct((ka, d), BF16)] * 2
    return pl.pallas_call(
        _outproj_kernel,
        grid=(m // tm, d // tn),
        in_specs=[pl.BlockSpec((tm, tn), lambda i, j: (i, j)),
                  pl.BlockSpec((tm, ka), lambda i, j: (i, 0)),
                  pl.BlockSpec((tm, ka), lambda i, j: (i, 0)),
                  w_tile(0), w_tile(b_block)],
        out_specs=out_specs,
        out_shape=out_shape,
        compiler_params=_params("parallel", "arbitrary"),
        name="outproj_convert" if convert else "outproj",
    )(x, ya, yb, w_a, w_b)


def _mlp_kernel(x_hbm, n2_ref, wu_ref, wd_ref, nf_ref, o_ref, *refs, final_norm):
    h_ref, sem = refs[-2:]
    convert = len(refs) > 2
    i, f = pl.program_id(0), pl.program_id(1)
    tm = o_ref.shape[0]

    @pl.when(f == 0)
    def _():
        load = pltpu.make_async_copy(x_hbm.at[pl.ds(pl.multiple_of(i * tm, tm), tm), :], o_ref, sem)
        load.start()
        load.wait()

        def norm_rows(rs):
            x = o_ref[rs, :]
            h_ref[rs, :] = (x * _rms_scale(x) * n2_ref[...]).astype(h_ref.dtype)
        _for_row_chunks(tm, norm_rows)

    wu = wu_ref[...]
    if convert:
        wu = wu.astype(BF16)
        refs[0][...] = wu
    hid = jnp.dot(h_ref[...], wu, preferred_element_type=F32)
    hid = jnp.square(jnp.maximum(hid, 0.0)).astype(BF16)
    for n in range(o_ref.shape[1] // MLP_ACC_COLS):
        sl = slice(n * MLP_ACC_COLS, (n + 1) * MLP_ACC_COLS)
        wd = wd_ref[:, sl]
        if convert:
            wd = wd.astype(BF16)
            refs[1][:, sl] = wd
        o_ref[:, sl] += jnp.dot(hid, wd, preferred_element_type=F32)

    if final_norm:
        @pl.when(f == pl.num_programs(1) - 1)
        def _():
            def norm_rows(rs):
                x = o_ref[rs, :]
                o_ref[rs, :] = x * _rms_scale(x) * nf_ref[...]
            _for_row_chunks(o_ref.shape[0], norm_rows)


def mlp(x, norm2, w_up, w_down, norm_f, *, final_norm, convert, tm, tf):
    m, d = x.shape
    ff = w_up.shape[1]
    once = pl.Buffered(1)
    up_tile = pl.BlockSpec((d, tf), lambda i, f: (0, f))
    down_tile = pl.BlockSpec((tf, d), lambda i, f: (f, 0))
    out_specs = [pl.BlockSpec((tm, d), lambda i, f: (i, 0))]
    out_shape = [jax.ShapeDtypeStruct((m, d), F32)]
    if convert:
        assert m == tm, "weight copies are written once per row block"
        out_specs += [up_tile, down_tile]
        out_shape += [jax.ShapeDtypeStruct((d, ff), BF16), jax.ShapeDtypeStruct((ff, d), BF16)]
    kern = functools.partial(_mlp_kernel, final_norm=final_norm)
    return pl.pallas_call(
        kern,
        grid=(m // tm, ff // tf),
        in_specs=[pl.BlockSpec(memory_space=pl.ANY),
                  pl.BlockSpec((1, d), lambda i, f: (0, 0), pipeline_mode=once),
                  up_tile, down_tile,
                  pl.BlockSpec((1, d), lambda i, f: (0, 0), pipeline_mode=once)],
        out_specs=out_specs,
        out_shape=out_shape,
        scratch_shapes=[pltpu.VMEM((tm, d), BF16), pltpu.SemaphoreType.DMA(())],
        compiler_params=_params("parallel", "arbitrary"),
        name="mlp_convert" if convert else "mlp",
    )(x, norm2.reshape(1, d), w_up, w_down, norm_f.reshape(1, d))


def _mlp_stream_kernel(x_hbm, xn_ref, xc_ref, n2_ref, wu_ref, wd_ref, nf_ref, o_ref, h_ref, warm_ref, sem,
                       *, final_norm):
    i, f = pl.program_id(0), pl.program_id(1)
    tm, d = o_ref.shape
    step_rows = xn_ref.shape[0]
    slot = i % 2
    norm = lambda x: (x * _rms_scale(x) * n2_ref[...]).astype(h_ref.dtype)

    @pl.when(jnp.logical_and(i == 0, f == 0))
    def _():
        def store_normed(rs, x):
            h_ref[0, rs, :] = norm(x)
        _warm_up_rows(x_hbm, warm_ref, sem, tm, store_normed)

    def step(first):
        rs = pl.ds(pl.multiple_of(f * step_rows, step_rows), step_rows)
        if not first:
            o_ref[rs, :] += xc_ref[...]
        hid = jnp.dot(h_ref[slot], wu_ref[...], preferred_element_type=F32)
        hid = jnp.square(jnp.maximum(hid, 0.0)).astype(BF16)
        for n in range(d // MLP_ACC_COLS):
            sl = slice(n * MLP_ACC_COLS, (n + 1) * MLP_ACC_COLS)
            part = jnp.dot(hid, wd_ref[:, sl], preferred_element_type=F32)
            if first:
                o_ref[:, sl] = part
            else:
                o_ref[:, sl] += part
        if first:
            o_ref[rs, :] += xc_ref[...]
        h_ref[1 - slot, rs, :] = norm(xn_ref[...])

    pl.when(f == 0)(lambda: step(True))
    pl.when(f > 0)(lambda: step(False))

    if final_norm:
        @pl.when(f == pl.num_programs(1) - 1)
        def _():
            def norm_rows(rs):
                x = o_ref[rs, :]
                o_ref[rs, :] = x * _rms_scale(x) * nf_ref[...]
            _for_row_chunks(tm, norm_rows)


def mlp_stream(x, norm2, w_up, w_down, norm_f, *, final_norm, tm, tf):
    m, d = x.shape
    ff = w_up.shape[1]
    n_row_blocks, n_f = m // tm, ff // tf
    step_rows = tm // n_f
    once = pl.Buffered(1)
    kern = functools.partial(_mlp_stream_kernel, final_norm=final_norm)
    return pl.pallas_call(
        kern,
        grid=(n_row_blocks, n_f),
        in_specs=[pl.BlockSpec(memory_space=pl.ANY),
                  pl.BlockSpec((step_rows, d), lambda i, f: (jnp.minimum(i + 1, n_row_blocks - 1) * n_f + f, 0)),
                  pl.BlockSpec((step_rows, d), lambda i, f: (i * n_f + f, 0)),
                  pl.BlockSpec((1, d), lambda i, f: (0, 0), pipeline_mode=once),
                  pl.BlockSpec((d, tf), lambda i, f: (0, f)),
                  pl.BlockSpec((tf, d), lambda i, f: (f, 0)),
                  pl.BlockSpec((1, d), lambda i, f: (0, 0), pipeline_mode=once)],
        out_specs=pl.BlockSpec((tm, d), lambda i, f: (i, 0), pipeline_mode=once),
        out_shape=jax.ShapeDtypeStruct((m, d), F32),
        scratch_shapes=[pltpu.VMEM((2, tm, d), BF16), pltpu.VMEM((MLP_WARM_ROWS, d), F32),
                        pltpu.SemaphoreType.DMA(())],
        compiler_params=_params("arbitrary", "arbitrary"),
        name="mlp",
    )(x, x, x, norm2.reshape(1, d), w_up, w_down, norm_f.reshape(1, d))


def _mix_heads(x3, s0, lb_logits, p, *, layer, emit_v, convert, side_weights=()):
    ns, t, d = x3.shape
    m = ns * t
    x = x3.reshape(m, d)
    span = min(t, GMLP_CHUNK)
    small = dict(norm_v=p["norm_v"], lb_logits=lb_logits, w_s=p["w_s"], b_s=p["b_s"])
    w_in_bf = None
    if convert:
        outs, w_in_bf = inproj_convert(x, p["norm1"], p["w_in"], **small, span=span, layer=layer,
                                       emit_v=emit_v, tk=1024)
    else:
        outs = inproj(x, p["norm1"], p["w_in"], **small, span=span, layer=layer, emit_v=emit_v,
                      tm=min(m, 1024))
    ya, (q, f, i, g) = outs[0], outs[-4:]
    width = ya.shape[1]
    if t >= 2048:
        sb, tb = 1, 2048
    else:
        sb, tb = ns, t
    per_stream = lambda a: a.reshape(ns, t, width)
    yb3, s_new, *side_bf = hgrn(per_stream(q), per_stream(f), per_stream(i), per_stream(g), p["norm_o"], s0,
                                blk=min(t, CHUNK), sb=sb, tb=tb, side_weights=side_weights)
    v_act = per_stream(outs[1]) if emit_v else None
    return x, ya, yb3.reshape(m, width), s_new, v_act, w_in_bf, side_bf


def kernel(x_prompt, x_sample, state_hgrn, norm1, w_in, w_s, b_s, norm_v, lb_logits, norm_o,
           w_out, norm2, w_up, w_down, norm_f):
    depth = w_in.shape[0]
    lb_logits = lb_logits.astype(F32)
    xp, xs = x_prompt, x_sample
    s_prompt, s_sample, v_sample = [], [], []
    for l in range(depth):
        p = dict(norm1=norm1[l], w_in=w_in[l], w_s=w_s[l], b_s=b_s[l], norm_v=norm_v[l], norm_o=norm_o[l])
        final_norm = l == depth - 1
        rows_s, ya_s, yb_s, ss, vs, w_in_bf, _ = _mix_heads(xs, state_hgrn[l], lb_logits, p, layer=l,
                                                             emit_v=True, convert=True)
        rows_p, ya_p, yb_p, sp, _, _, (w_up_bf, w_down_bf) = _mix_heads(
            xp, None, lb_logits, {**p, "w_in": w_in_bf}, layer=l, emit_v=False, convert=False,
            side_weights=(w_up[l], w_down[l]))
        x1_s, *w_out_bf = outproj(rows_s, ya_s, yb_s, w_out[l], w_out[l], 1, convert=True,
                                  tm=rows_s.shape[0], tn=512)
        x2_s, = mlp(x1_s, norm2[l], w_up_bf, w_down_bf, norm_f, final_norm=final_norm, convert=False,
                    tm=rows_s.shape[0], tf=512)
        x1_p, = outproj(rows_p, ya_p, yb_p, *w_out_bf, 0, convert=False, tm=1024, tn=1024)
        x2_p = mlp_stream(x1_p, norm2[l], w_up_bf, w_down_bf, norm_f, final_norm=final_norm, tm=1024, tf=512)
        xs, xp = x2_s.reshape(xs.shape), x2_p.reshape(xp.shape)
        s_prompt.append(sp)
        s_sample.append(ss)
        v_sample.append(vs)
    return (xp, xs, jnp.stack(s_prompt), jnp.stack(s_sample), jnp.stack(v_sample))
```

```python
import functools

import jax
import jax.numpy as jnp
from jax import lax
from jax.experimental import pallas as pl
from jax.experimental.pallas import tpu as pltpu

EPS = 1e-6
CHUNK = 64
GMLP_CHUNK = 128
HEADS_A = 8
N_GROUPS = 6
DK_B = 128
MLP_ACC_COLS = 1024
NORM_ROWS = 16
MLP_WARM_ROWS = 128
BF16 = jnp.bfloat16
F32 = jnp.float32

V7X_VMEM_BYTES = 64 * 1024 * 1024
VMEM_LIMIT_BYTES = V7X_VMEM_BYTES - 4 * 1024 * 1024


def _params(*semantics):
    return pltpu.CompilerParams(dimension_semantics=semantics, vmem_limit_bytes=VMEM_LIMIT_BYTES)


def _rms_scale(x):
    return lax.rsqrt(jnp.mean(x * x, axis=-1, keepdims=True) + EPS)


def _for_row_chunks(n_rows, body):
    for r in range(0, n_rows, NORM_ROWS):
        body(slice(r, r + NORM_ROWS))


def _inproj_epilogue(pre, nv_ref, lb_ref, ws_ref, bs_ref, out_refs, *, span, layer, emit_v):
    if emit_v:
        ya_ref, v_ref, q_ref, f_ref, i_ref, g_ref = out_refs
    else:
        ya_ref, q_ref, f_ref, i_ref, g_ref = out_refs

    u = jax.nn.gelu(pre(0))
    a = jax.nn.gelu(pre(1))
    v = a * _rms_scale(a) * nv_ref[...]
    if emit_v:
        v_ref[...] = v
    row = lax.broadcasted_iota(jnp.int32, (span, span), 0)
    col = lax.broadcasted_iota(jnp.int32, (span, span), 1)
    w = jnp.where(col // CHUNK <= row // CHUNK, ws_ref[:span, :span], 0.0).astype(BF16)
    bias = bs_ref[:span, :]
    for c in range(u.shape[0] // span):
        sl = slice(c * span, (c + 1) * span)
        mixed = jnp.dot(w, v[sl, :].astype(BF16), preferred_element_type=F32) + bias
        ya_ref[sl, :] = (u[sl, :] * mixed).astype(ya_ref.dtype)

    qz = pre(2)
    q_ref[...] = (qz * jax.nn.sigmoid(qz)).astype(q_ref.dtype)

    lg = lb_ref[...]
    e = jnp.exp(lg - jnp.max(lg, axis=0, keepdims=True))
    sm = e / jnp.sum(e, axis=0, keepdims=True)
    lb = jnp.sum(sm[:layer + 1, :], axis=0, keepdims=True)
    f_ref[...] = lb + (1.0 - lb) * jax.nn.sigmoid(pre(3))

    i_ref[...] = pre(4).astype(i_ref.dtype)

    gz = pre(5)
    g_ref[...] = (gz * jax.nn.sigmoid(gz)).astype(g_ref.dtype)


def _warm_up_rows(x_hbm, warm_ref, sem, n_rows, store_normed):
    warm_rows = warm_ref.shape[0]
    for c in range(n_rows // warm_rows):
        load = pltpu.make_async_copy(x_hbm.at[pl.ds(c * warm_rows, warm_rows), :], warm_ref, sem)
        load.start()
        load.wait()
        for r in range(0, warm_rows, NORM_ROWS):
            store_normed(slice(c * warm_rows + r, c * warm_rows + r + NORM_ROWS), warm_ref[r:r + NORM_ROWS, :])


def _inproj_kernel(x_hbm, n1_ref, *refs, span, layer, emit_v):
    w_refs = refs[:N_GROUPS]
    nv_ref, lb_ref, ws_ref, bs_ref = refs[N_GROUPS:N_GROUPS + 4]
    out_refs = refs[N_GROUPS + 4:-3]
    h_ref, x_buf, sem = refs[-3:]
    i, j = pl.program_id(0), pl.program_id(1)
    tm = h_ref.shape[0]

    def x_rows_copy(block):
        rows = pl.ds(pl.multiple_of(block * tm, tm), tm)
        return pltpu.make_async_copy(x_hbm.at[rows, :], x_buf, sem)

    @pl.when(j == 0)
    def _():
        @pl.when(i == 0)
        def _():
            x_rows_copy(0).start()
        x_rows_copy(i).wait()

        def norm_rows(rs):
            x = x_buf[rs, :]
            h_ref[rs, :] = (x * _rms_scale(x) * n1_ref[...]).astype(h_ref.dtype)
        _for_row_chunks(tm, norm_rows)

    @pl.when(jnp.logical_and(j == 1, i + 1 < pl.num_programs(0)))
    def _():
        x_rows_copy(i + 1).start()

    pre = lambda g: jnp.dot(h_ref[...], w_refs[g][...], preferred_element_type=F32)
    _inproj_epilogue(pre, nv_ref, lb_ref, ws_ref, bs_ref, out_refs, span=span, layer=layer, emit_v=emit_v)


def _inproj_convert_kernel(x_ref, n1_ref, *refs, span, layer, emit_v):
    w_refs = refs[:N_GROUPS]
    nv_ref, lb_ref, ws_ref, bs_ref = refs[N_GROUPS:N_GROUPS + 4]
    n_act = 6 if emit_v else 5
    out_refs = refs[N_GROUPS + 4:N_GROUPS + 4 + n_act]
    wb_refs = refs[N_GROUPS + 4 + n_act:-2]
    h_ref, acc_ref = refs[-2:]
    j, k = pl.program_id(1), pl.program_id(2)
    kt, _, tk = h_ref.shape

    @pl.when(jnp.logical_and(j == 0, k == 0))
    def _():
        def norm_rows(rs):
            x = x_ref[rs, :]
            h = (x * _rms_scale(x) * n1_ref[...]).astype(h_ref.dtype)
            for kk in range(kt):
                h_ref[kk, rs, :] = h[:, kk * tk:(kk + 1) * tk]
        _for_row_chunks(x_ref.shape[0], norm_rows)

    @pl.when(k == 0)
    def _():
        acc_ref[...] = jnp.zeros_like(acc_ref)

    for g in range(N_GROUPS):
        wb = w_refs[g][...].astype(BF16)
        wb_refs[g][...] = wb
        acc_ref[g] += jnp.dot(h_ref[k], wb, preferred_element_type=F32)

    @pl.when(k == kt - 1)
    def _():
        _inproj_epilogue(lambda g: acc_ref[g], nv_ref, lb_ref, ws_ref, bs_ref, out_refs,
                         span=span, layer=layer, emit_v=emit_v)


def inproj(x, norm1, w_groups, norm_v, lb_logits, w_s, b_s, *, span, layer, emit_v, tm):
    m, d = x.shape
    width = w_groups[0].shape[1]
    tn = width // HEADS_A
    tile = pl.BlockSpec((tm, tn), lambda i, j: (i, j))
    out_dtypes = [BF16] + ([F32] if emit_v else []) + [BF16, F32, BF16, BF16]
    kern = functools.partial(_inproj_kernel, span=span, layer=layer, emit_v=emit_v)
    return pl.pallas_call(
        kern,
        grid=(m // tm, width // tn),
        in_specs=[pl.BlockSpec(memory_space=pl.ANY),
                  pl.BlockSpec((1, d), lambda i, j: (0, 0))]
                 + [pl.BlockSpec((d, tn), lambda i, j: (0, j))] * N_GROUPS
                 + [pl.BlockSpec((1, tn), lambda i, j: (0, j)),
                    pl.BlockSpec((lb_logits.shape[0], tn), lambda i, j: (0, j)),
                    pl.BlockSpec((None, GMLP_CHUNK, GMLP_CHUNK), lambda i, j: (j, 0, 0)),
                    pl.BlockSpec((None, GMLP_CHUNK, 1), lambda i, j: (j, 0, 0))],
        out_specs=[tile] * len(out_dtypes),
        out_shape=[jax.ShapeDtypeStruct((m, width), dt) for dt in out_dtypes],
        scratch_shapes=[pltpu.VMEM((tm, d), BF16), pltpu.VMEM((tm, d), F32), pltpu.SemaphoreType.DMA(())],
        compiler_params=_params("arbitrary", "arbitrary"),
        name="inproj",
    )(x, norm1.reshape(1, d), *w_groups, norm_v.reshape(1, width), lb_logits,
      w_s, b_s.reshape(HEADS_A, GMLP_CHUNK, 1))


def inproj_convert(x, norm1, w_in, norm_v, lb_logits, w_s, b_s, *, span, layer, emit_v, tk):
    m, d = x.shape
    width = w_in.shape[1] // N_GROUPS
    tn = width // HEADS_A
    nt, kt = width // tn, d // tk
    w_spec = lambda g: pl.BlockSpec((tk, tn), lambda i, j, k: (k, g * nt + j))
    tile = pl.BlockSpec((m, tn), lambda i, j, k: (i, j))
    out_dtypes = [BF16] + ([F32] if emit_v else []) + [BF16, F32, BF16, BF16]
    kern = functools.partial(_inproj_convert_kernel, span=span, layer=layer, emit_v=emit_v)
    outs = pl.pallas_call(
        kern,
        grid=(1, nt, kt),
        in_specs=[pl.BlockSpec((m, d), lambda i, j, k: (i, 0)),
                  pl.BlockSpec((1, d), lambda i, j, k: (0, 0))]
                 + [w_spec(g) for g in range(N_GROUPS)]
                 + [pl.BlockSpec((1, tn), lambda i, j, k: (0, j)),
                    pl.BlockSpec((lb_logits.shape[0], tn), lambda i, j, k: (0, j)),
                    pl.BlockSpec((None, GMLP_CHUNK, GMLP_CHUNK), lambda i, j, k: (j, 0, 0)),
                    pl.BlockSpec((None, GMLP_CHUNK, 1), lambda i, j, k: (j, 0, 0))],
        out_specs=[tile] * len(out_dtypes)
                  + [pl.BlockSpec((tk, tn), lambda i, j, k: (k, j))] * N_GROUPS,
        out_shape=[jax.ShapeDtypeStruct((m, width), dt) for dt in out_dtypes]
                  + [jax.ShapeDtypeStruct((d, width), BF16)] * N_GROUPS,
        scratch_shapes=[pltpu.VMEM((kt, m, tk), BF16), pltpu.VMEM((N_GROUPS, m, tn), F32)],
        compiler_params=_params("arbitrary", "arbitrary", "arbitrary"),
        name="inproj_convert",
    )(x, norm1.reshape(1, d), *([w_in] * N_GROUPS), norm_v.reshape(1, width), lb_logits,
      w_s, b_s.reshape(HEADS_A, GMLP_CHUNK, 1))
    return outs[:len(out_dtypes)], outs[len(out_dtypes):]


def _split2_bf16(x):
    hi = x.astype(BF16)
    lo = (x - hi.astype(F32)).astype(BF16)
    return hi, lo


def _hgrn_kernel(q_ref, f_ref, i_ref, g_ref, no_ref, s0_ref, *refs, blk, has_s0):
    n_side = (len(refs) - 3) // 2
    side_in, side_out = refs[:n_side], refs[n_side + 2:-1]
    y_ref, s_ref, st_ref = refs[n_side], refs[n_side + 1], refs[-1]
    tb = pl.program_id(2)
    n_streams, tokens, _ = q_ref.shape
    n_chunks = tokens // blk
    items = [(s, c) for s in range(n_streams) for c in range(n_chunks)]
    rows = lambda c: slice(c * blk, (c + 1) * blk)

    @pl.when(tb == 0)
    def _():
        if has_s0:
            for s in range(n_streams):
                st_ref[s] = s0_ref[s].T
        else:
            st_ref[...] = jnp.zeros_like(st_ref)

    row = lax.broadcasted_iota(jnp.int32, (blk, blk), 0)
    col = lax.broadcasted_iota(jnp.int32, (blk, blk), 1)
    causal = col <= row
    tri = causal.astype(BF16)
    gain = no_ref[...]
    nt = (((1,), (1,)), ((), ()))
    tn = (((0,), (0,)), ((), ()))

    for w_ref, wb_ref in zip(side_in, side_out):
        wb_ref[...] = w_ref[...].astype(wb_ref.dtype)

    cum = {}
    for s, c in items:
        hi, lo = _split2_bf16(jnp.log(f_ref[s, rows(c), :]))
        cum[s, c] = (jnp.dot(tri, hi, preferred_element_type=F32)
                     + jnp.dot(tri, lo, preferred_element_type=F32))

    q_rel, k_rel, q_abs, k_end, decay, val = {}, {}, {}, {}, {}, {}
    for s, c in items:
        b = cum[s, c]
        b_mid = b[blk // 2:blk // 2 + 1, :]
        b_end = b[blk - 1:blk, :]
        qr = q_ref[s, rows(c), :] * jnp.exp(b - b_mid)
        kr = (1.0 - f_ref[s, rows(c), :]) * jnp.exp(b_mid - b)
        q_rel[s, c] = qr.astype(BF16)
        k_rel[s, c] = kr.astype(BF16)
        q_abs[s, c] = (qr * jnp.exp(b_mid)).astype(BF16)
        k_end[s, c] = (kr * jnp.exp(b_end - b_mid)).astype(BF16)
        decay[s, c] = jnp.exp(b_end)
        val[s, c] = i_ref[s, rows(c), :].astype(BF16)

    scores = {}
    for it in items:
        sc = lax.dot_general(q_rel[it], k_rel[it], nt, preferred_element_type=F32)
        scores[it] = jnp.where(causal, sc, 0.0).astype(BF16)

    out, delta = {}, {}
    for it in items:
        out[it] = jnp.dot(scores[it], val[it], preferred_element_type=F32)
        delta[it] = lax.dot_general(val[it], k_end[it], tn, preferred_element_type=F32)

    for s in range(n_streams):
        st = st_ref[s]
        for c in range(n_chunks):
            out[s, c] = out[s, c] + lax.dot_general(q_abs[s, c], st.astype(BF16), nt,
                                                    preferred_element_type=F32)
            st = st * decay[s, c] + delta[s, c]
        st_ref[s] = st

    for s, c in items:
        o = out[s, c]
        y_ref[s, rows(c), :] = (o * _rms_scale(o) * gain * g_ref[s, rows(c), :]).astype(y_ref.dtype)

    @pl.when(tb == pl.num_programs(2) - 1)
    def _():
        for s in range(n_streams):
            s_ref[s] = st_ref[s].T


def hgrn(q, f, i, g, norm_o, s0, *, blk, sb, tb, side_weights=()):
    ns, t, width = q.shape
    heads = width // DK_B
    has_s0 = s0 is not None
    grid = (ns // sb, heads, t // tb)
    n_steps = grid[0] * grid[1] * grid[2]
    tok_spec = pl.BlockSpec((sb, tb, DK_B), lambda b, h, c: (b, c, h))
    state_spec = pl.BlockSpec((sb, None, DK_B, DK_B), lambda b, h, c: (b, h, 0, 0))
    side_specs = []
    for w in side_weights:
        assert w.shape[0] % (n_steps * NORM_ROWS) == 0, "side windows must be whole packed bf16 row tiles"
        side_specs.append(pl.BlockSpec((w.shape[0] // n_steps, w.shape[1]),
                                       lambda b, h, c: ((b * grid[1] + h) * grid[2] + c, 0)))
    if not has_s0:
        s0 = jnp.zeros((ns, heads, DK_B, DK_B), F32)
    kern = functools.partial(_hgrn_kernel, blk=blk, has_s0=has_s0)
    return pl.pallas_call(
        kern,
        grid=grid,
        in_specs=[tok_spec, tok_spec, tok_spec, tok_spec,
                  pl.BlockSpec((None, 1, DK_B), lambda b, h, c: (h, 0, 0)),
                  state_spec] + side_specs,
        out_specs=[tok_spec, state_spec] + side_specs,
        out_shape=[jax.ShapeDtypeStruct((ns, t, width), BF16),
                   jax.ShapeDtypeStruct((ns, heads, DK_B, DK_B), F32)]
                  + [jax.ShapeDtypeStruct(w.shape, BF16) for w in side_weights],
        scratch_shapes=[pltpu.VMEM((sb, DK_B, DK_B), F32)],
        compiler_params=_params("parallel", "parallel", "arbitrary"),
        name="hgrn",
    )(q, f, i, g, norm_o.reshape(heads, 1, DK_B), s0, *side_weights)


def _outproj_kernel(x_ref, ya_ref, yb_ref, wa_ref, wb_ref, o_ref, *wcopy_refs):
    wa, wb = wa_ref[...], wb_ref[...]
    if wcopy_refs:
        wa, wb = wa.astype(BF16), wb.astype(BF16)
        wcopy_refs[0][...] = wa
        wcopy_refs[1][...] = wb
    o_ref[...] = (x_ref[...]
                  + jnp.dot(ya_ref[...], wa, preferred_element_type=F32)
                  + jnp.dot(yb_ref[...], wb, preferred_element_type=F32))


def outproj(x, ya, yb, w_a, w_b, b_block, *, convert, tm, tn):
    m, d = x.shape
    ka = ya.shape[1]
    w_tile = lambda blk: pl.BlockSpec((ka, tn), lambda i, j: (blk, j))
    out_specs = [pl.BlockSpec((tm, tn), lambda i, j: (i, j))]
    out_shape = [jax.ShapeDtypeStruct((m, d), F32)]
    if convert:
        assert m == tm, "weight copies are written once per row block"
        out_specs += [w_tile(0)] * 2
        out_shape += [jax.ShapeDtypeStruct((ka, d), BF16)] * 2
    return pl.pallas_call(
        _outproj_kernel,
        grid=(m // tm, d // tn),
        in_specs=[pl.BlockSpec((tm, tn), lambda i, j: (i, j)),
                  pl.BlockSpec((tm, ka), lambda i, j: (i, 0)),
                  pl.BlockSpec((tm, ka), lambda i, j: (i, 0)),
                  w_tile(0), w_tile(b_block)],
        out_specs=out_specs,
        out_shape=out_shape,
        compiler_params=_params("parallel", "arbitrary"),
        name="outproj_convert" if convert else "outproj",
    )(x, ya, yb, w_a, w_b)


def _mlp_kernel(x_hbm, n2_ref, wu_ref, wd_ref, nf_ref, o_ref, wub_ref, h_ref, sem, *, final_norm):
    i, f = pl.program_id(0), pl.program_id(1)
    tm = o_ref.shape[0]

    @pl.when(f == 0)
    def _():
        load = pltpu.make_async_copy(x_hbm.at[pl.ds(pl.multiple_of(i * tm, tm), tm), :], o_ref, sem)
        load.start()
        load.wait()

        def norm_rows(rs):
            x = o_ref[rs, :]
            h_ref[rs, :] = (x * _rms_scale(x) * n2_ref[...]).astype(h_ref.dtype)
        _for_row_chunks(tm, norm_rows)

    wu = wu_ref[...].astype(BF16)
    wub_ref[...] = wu
    hid = jnp.dot(h_ref[...], wu, preferred_element_type=F32)
    hid = jnp.square(jnp.maximum(hid, 0.0)).astype(BF16)
    for n in range(o_ref.shape[1] // MLP_ACC_COLS):
        sl = slice(n * MLP_ACC_COLS, (n + 1) * MLP_ACC_COLS)
        o_ref[:, sl] += jnp.dot(hid, wd_ref[:, sl], preferred_element_type=F32)

    if final_norm:
        @pl.when(f == pl.num_programs(1) - 1)
        def _():
            def norm_rows(rs):
                x = o_ref[rs, :]
                o_ref[rs, :] = x * _rms_scale(x) * nf_ref[...]
            _for_row_chunks(o_ref.shape[0], norm_rows)


def mlp_convert_up(x, norm2, w_up, w_down, norm_f, *, final_norm, tf):
    m, d = x.shape
    ff = w_up.shape[1]
    once = pl.Buffered(1)
    up_tile = pl.BlockSpec((d, tf), lambda i, f: (0, f))
    kern = functools.partial(_mlp_kernel, final_norm=final_norm)
    return pl.pallas_call(
        kern,
        grid=(1, ff // tf),
        in_specs=[pl.BlockSpec(memory_space=pl.ANY),
                  pl.BlockSpec((1, d), lambda i, f: (0, 0), pipeline_mode=once),
                  up_tile,
                  pl.BlockSpec((tf, d), lambda i, f: (f, 0)),
                  pl.BlockSpec((1, d), lambda i, f: (0, 0), pipeline_mode=once)],
        out_specs=[pl.BlockSpec((m, d), lambda i, f: (i, 0)), up_tile],
        out_shape=[jax.ShapeDtypeStruct((m, d), F32), jax.ShapeDtypeStruct((d, ff), BF16)],
        scratch_shapes=[pltpu.VMEM((m, d), BF16), pltpu.SemaphoreType.DMA(())],
        compiler_params=_params("arbitrary", "arbitrary"),
        name="mlp_convert_up",
    )(x, norm2.reshape(1, d), w_up, w_down, norm_f.reshape(1, d))


def _mlp_stream_kernel(x_hbm, xn_ref, xc_ref, n2_ref, wu_ref, wd_ref, nf_ref, o_ref, h_ref, warm_ref, sem,
                       *, final_norm):
    i, f = pl.program_id(0), pl.program_id(1)
    tm, d = o_ref.shape
    step_rows = xn_ref.shape[0]
    slot = i % 2
    norm = lambda x: (x * _rms_scale(x) * n2_ref[...]).astype(h_ref.dtype)

    @pl.when(jnp.logical_and(i == 0, f == 0))
    def _():
        def store_normed(rs, x):
            h_ref[0, rs, :] = norm(x)
        _warm_up_rows(x_hbm, warm_ref, sem, tm, store_normed)

    def step(first):
        rs = pl.ds(pl.multiple_of(f * step_rows, step_rows), step_rows)
        if not first:
            o_ref[rs, :] += xc_ref[...]
        hid = jnp.dot(h_ref[slot], wu_ref[...], preferred_element_type=F32)
        hid = jnp.square(jnp.maximum(hid, 0.0)).astype(BF16)
        for n in range(d // MLP_ACC_COLS):
            sl = slice(n * MLP_ACC_COLS, (n + 1) * MLP_ACC_COLS)
            part = jnp.dot(hid, wd_ref[:, sl], preferred_element_type=F32)
            if first:
                o_ref[:, sl] = part
            else:
                o_ref[:, sl] += part
        if first:
            o_ref[rs, :] += xc_ref[...]
        h_ref[1 - slot, rs, :] = norm(xn_ref[...])

    pl.when(f == 0)(lambda: step(True))
    pl.when(f > 0)(lambda: step(False))

    if final_norm:
        @pl.when(f == pl.num_programs(1) - 1)
        def _():
            def norm_rows(rs):
                x = o_ref[rs, :]
                o_ref[rs, :] = x * _rms_scale(x) * nf_ref[...]
            _for_row_chunks(tm, norm_rows)


def mlp_stream(x, norm2, w_up, w_down, norm_f, *, final_norm, tm, tf):
    m, d = x.shape
    ff = w_up.shape[1]
    n_row_blocks, n_f = m // tm, ff // tf
    step_rows = tm // n_f
    once = pl.Buffered(1)
    kern = functools.partial(_mlp_stream_kernel, final_norm=final_norm)
    return pl.pallas_call(
        kern,
        grid=(n_row_blocks, n_f),
        in_specs=[pl.BlockSpec(memory_space=pl.ANY),
                  pl.BlockSpec((step_rows, d), lambda i, f: (jnp.minimum(i + 1, n_row_blocks - 1) * n_f + f, 0)),
                  pl.BlockSpec((step_rows, d), lambda i, f: (i * n_f + f, 0)),
                  pl.BlockSpec((1, d), lambda i, f: (0, 0), pipeline_mode=once),
                  pl.BlockSpec((d, tf), lambda i, f: (0, f)),
                  pl.BlockSpec((tf, d), lambda i, f: (f, 0)),
                  pl.BlockSpec((1, d), lambda i, f: (0, 0), pipeline_mode=once)],
        out_specs=pl.BlockSpec((tm, d), lambda i, f: (i, 0), pipeline_mode=once),
        out_shape=jax.ShapeDtypeStruct((m, d), F32),
        scratch_shapes=[pltpu.VMEM((2, tm, d), BF16), pltpu.VMEM((MLP_WARM_ROWS, d), F32),
                        pltpu.SemaphoreType.DMA(())],
        compiler_params=_params("arbitrary", "arbitrary"),
        name="mlp",
    )(x, x, x, norm2.reshape(1, d), w_up, w_down, norm_f.reshape(1, d))


def _mix_heads(x3, s0, lb_logits, p, *, layer, emit_v, convert, side_weights=()):
    ns, t, d = x3.shape
    m = ns * t
    x = x3.reshape(m, d)
    span = min(t, GMLP_CHUNK)
    small = dict(norm_v=p["norm_v"], lb_logits=lb_logits, w_s=p["w_s"], b_s=p["b_s"])
    w_in_bf = None
    if convert:
        outs, w_in_bf = inproj_convert(x, p["norm1"], p["w_in"], **small, span=span, layer=layer,
                                       emit_v=emit_v, tk=1024)
    else:
        outs = inproj(x, p["norm1"], p["w_in"], **small, span=span, layer=layer, emit_v=emit_v,
                      tm=min(m, 1024))
    ya, (q, f, i, g) = outs[0], outs[-4:]
    width = ya.shape[1]
    if t >= 2048:
        sb, tb = 1, 2048
    else:
        sb, tb = ns, t
    per_stream = lambda a: a.reshape(ns, t, width)
    yb3, s_new, *side_bf = hgrn(per_stream(q), per_stream(f), per_stream(i), per_stream(g), p["norm_o"], s0,
                                blk=min(t, CHUNK), sb=sb, tb=tb, side_weights=side_weights)
    v_act = per_stream(outs[1]) if emit_v else None
    return x, ya, yb3.reshape(m, width), s_new, v_act, w_in_bf, side_bf


def kernel(x_prompt, x_sample, state_hgrn, norm1, w_in, w_s, b_s, norm_v, lb_logits, norm_o,
           w_out, norm2, w_up, w_down, norm_f):
    depth = w_in.shape[0]
    lb_logits = lb_logits.astype(F32)
    xp, xs = x_prompt, x_sample
    s_prompt, s_sample, v_sample = [], [], []
    for l in range(depth):
        p = dict(norm1=norm1[l], w_in=w_in[l], w_s=w_s[l], b_s=b_s[l], norm_v=norm_v[l], norm_o=norm_o[l])
        final_norm = l == depth - 1
        rows_s, ya_s, yb_s, ss, vs, w_in_bf, _ = _mix_heads(xs, state_hgrn[l], lb_logits, p, layer=l,
                                                             emit_v=True, convert=True)
        rows_p, ya_p, yb_p, sp, _, _, (w_down_bf,) = _mix_heads(
            xp, None, lb_logits, {**p, "w_in": w_in_bf}, layer=l, emit_v=False, convert=False,
            side_weights=(w_down[l],))
        x1_s, *w_out_bf = outproj(rows_s, ya_s, yb_s, w_out[l], w_out[l], 1, convert=True,
                                  tm=rows_s.shape[0], tn=512)
        x2_s, w_up_bf = mlp_convert_up(x1_s, norm2[l], w_up[l], w_down_bf, norm_f, final_norm=final_norm, tf=512)
        x1_p, = outproj(rows_p, ya_p, yb_p, *w_out_bf, 0, convert=False, tm=1024, tn=1024)
        x2_p = mlp_stream(x1_p, norm2[l], w_up_bf, w_down_bf, norm_f, final_norm=final_norm, tm=1024, tf=512)
        xs, xp = x2_s.reshape(xs.shape), x2_p.reshape(xp.shape)
        s_prompt.append(sp)
        s_sample.append(ss)
        v_sample.append(vs)
    return (xp, xs, jnp.stack(s_prompt), jnp.stack(s_sample), jnp.stack(v_sample))
```

```python
import functools

import jax
import jax.numpy as jnp
from jax import lax
from jax.experimental import pallas as pl
from jax.experimental.pallas import tpu as pltpu

EPS = 1e-6
CHUNK = 64
GMLP_CHUNK = 128
HEADS_A = 8
N_GROUPS = 6
DK_B = 128
MLP_ACC_COLS = 1024
NORM_ROWS = 16
F32_SUBLANES = 8
MLP_WARM_ROWS = 128
BF16 = jnp.bfloat16
F32 = jnp.float32

V7X_VMEM_BYTES = 64 * 1024 * 1024
VMEM_LIMIT_BYTES = V7X_VMEM_BYTES - 4 * 1024 * 1024


def _params(*semantics, **extra):
    return pltpu.CompilerParams(dimension_semantics=semantics, vmem_limit_bytes=VMEM_LIMIT_BYTES, **extra)


def _rms_scale(x):
    return lax.rsqrt(jnp.mean(x * x, axis=-1, keepdims=True) + EPS)


def _for_row_chunks(n_rows, body, chunk=NORM_ROWS):
    for r in range(0, n_rows, chunk):
        body(slice(r, r + chunk))


def _inproj_epilogue(pre, nv_ref, lb_ref, ws_ref, bs_ref, out_refs, *, span, layer, emit_v):
    if emit_v:
        ya_ref, v_ref, q_ref, f_ref, i_ref, g_ref = out_refs
    else:
        ya_ref, q_ref, f_ref, i_ref, g_ref = out_refs

    u = jax.nn.gelu(pre(0))
    a = jax.nn.gelu(pre(1))
    v = a * _rms_scale(a) * nv_ref[...]
    if emit_v:
        v_ref[...] = v
    row = lax.broadcasted_iota(jnp.int32, (span, span), 0)
    col = lax.broadcasted_iota(jnp.int32, (span, span), 1)
    w = jnp.where(col // CHUNK <= row // CHUNK, ws_ref[:span, :span], 0.0).astype(BF16)
    bias = bs_ref[:span, :]
    for c in range(u.shape[0] // span):
        sl = slice(c * span, (c + 1) * span)
        mixed = jnp.dot(w, v[sl, :].astype(BF16), preferred_element_type=F32) + bias
        ya_ref[sl, :] = (u[sl, :] * mixed).astype(ya_ref.dtype)

    qz = pre(2)
    q_ref[...] = (qz * jax.nn.sigmoid(qz)).astype(q_ref.dtype)

    lg = lb_ref[...]
    e = jnp.exp(lg - jnp.max(lg, axis=0, keepdims=True))
    sm = e / jnp.sum(e, axis=0, keepdims=True)
    lb = jnp.sum(sm[:layer + 1, :], axis=0, keepdims=True)
    f_ref[...] = lb + (1.0 - lb) * jax.nn.sigmoid(pre(3))

    gz = pre(5)
    g_ref[...] = (gz * jax.nn.sigmoid(gz)).astype(g_ref.dtype)

    i_ref[...] = pre(4).astype(i_ref.dtype)


def _warm_up_rows(x_hbm, warm_ref, sem, n_rows, store_normed):
    warm_rows = warm_ref.shape[0]
    for c in range(n_rows // warm_rows):
        load = pltpu.make_async_copy(x_hbm.at[pl.ds(c * warm_rows, warm_rows), :], warm_ref, sem)
        load.start()
        load.wait()
        for r in range(0, warm_rows, NORM_ROWS):
            store_normed(slice(c * warm_rows + r, c * warm_rows + r + NORM_ROWS), warm_ref[r:r + NORM_ROWS, :])


def _inproj_kernel(x_hbm, n1_ref, *refs, span, layer, emit_v):
    w_refs = refs[:N_GROUPS]
    nv_ref, lb_ref, ws_ref, bs_ref = refs[N_GROUPS:N_GROUPS + 4]
    out_refs = refs[N_GROUPS + 4:-3]
    h_ref, x_buf, sem = refs[-3:]
    i, j = pl.program_id(0), pl.program_id(1)
    tm = h_ref.shape[0]

    def x_rows_copy(block):
        rows = pl.ds(pl.multiple_of(block * tm, tm), tm)
        return pltpu.make_async_copy(x_hbm.at[rows, :], x_buf, sem)

    @pl.when(j == 0)
    def _():
        @pl.when(i == 0)
        def _():
            x_rows_copy(0).start()
        x_rows_copy(i).wait()

        def norm_rows(rs):
            x = x_buf[rs, :]
            h_ref[rs, :] = (x * _rms_scale(x) * n1_ref[...]).astype(h_ref.dtype)
        _for_row_chunks(tm, norm_rows)

    @pl.when(jnp.logical_and(j == 1, i + 1 < pl.num_programs(0)))
    def _():
        x_rows_copy(i + 1).start()

    pre = lambda g: jnp.dot(h_ref[...], w_refs[g][...], preferred_element_type=F32)
    _inproj_epilogue(pre, nv_ref, lb_ref, ws_ref, bs_ref, out_refs, span=span, layer=layer, emit_v=emit_v)


def _inproj_convert_kernel(x_ref, n1_ref, *refs, span, layer, emit_v):
    w_refs = refs[:N_GROUPS]
    nv_ref, lb_ref, ws_ref, bs_ref = refs[N_GROUPS:N_GROUPS + 4]
    n_act = 6 if emit_v else 5
    out_refs = refs[N_GROUPS + 4:N_GROUPS + 4 + n_act]
    wb_refs = refs[N_GROUPS + 4 + n_act:-2]
    h_ref, acc_ref = refs[-2:]
    j, k = pl.program_id(1), pl.program_id(2)
    kt, _, tk = h_ref.shape

    @pl.when(jnp.logical_and(j == 0, k == 0))
    def _():
        def norm_rows(rs):
            x = x_ref[rs, :]
            h = (x * _rms_scale(x) * n1_ref[...]).astype(h_ref.dtype)
            for kk in range(kt):
                h_ref[kk, rs, :] = h[:, kk * tk:(kk + 1) * tk]
        _for_row_chunks(x_ref.shape[0], norm_rows)

    @pl.when(k == 0)
    def _():
        acc_ref[...] = jnp.zeros_like(acc_ref)

    for g in range(N_GROUPS):
        wb = w_refs[g][...].astype(BF16)
        wb_refs[g][...] = wb
        acc_ref[g] += jnp.dot(h_ref[k], wb, preferred_element_type=F32)

    @pl.when(k == kt - 1)
    def _():
        _inproj_epilogue(lambda g: acc_ref[g], nv_ref, lb_ref, ws_ref, bs_ref, out_refs,
                         span=span, layer=layer, emit_v=emit_v)


def inproj(x, norm1, w_groups, norm_v, lb_logits, w_s, b_s, *, span, layer, emit_v, tm):
    m, d = x.shape
    width = w_groups[0].shape[1]
    tn = width // HEADS_A
    tile = pl.BlockSpec((tm, tn), lambda i, j: (i, j))
    out_dtypes = [BF16] + ([F32] if emit_v else []) + [BF16, F32, BF16, BF16]
    kern = functools.partial(_inproj_kernel, span=span, layer=layer, emit_v=emit_v)
    return pl.pallas_call(
        kern,
        grid=(m // tm, width // tn),
        in_specs=[pl.BlockSpec(memory_space=pl.ANY),
                  pl.BlockSpec((1, d), lambda i, j: (0, 0))]
                 + [pl.BlockSpec((d, tn), lambda i, j: (0, j))] * N_GROUPS
                 + [pl.BlockSpec((1, tn), lambda i, j: (0, j)),
                    pl.BlockSpec((lb_logits.shape[0], tn), lambda i, j: (0, j)),
                    pl.BlockSpec((None, GMLP_CHUNK, GMLP_CHUNK), lambda i, j: (j, 0, 0)),
                    pl.BlockSpec((None, GMLP_CHUNK, 1), lambda i, j: (j, 0, 0))],
        out_specs=[tile] * len(out_dtypes),
        out_shape=[jax.ShapeDtypeStruct((m, width), dt) for dt in out_dtypes],
        scratch_shapes=[pltpu.VMEM((tm, d), BF16), pltpu.VMEM((tm, d), F32), pltpu.SemaphoreType.DMA(())],
        compiler_params=_params("arbitrary", "arbitrary"),
        name="inproj",
    )(x, norm1.reshape(1, d), *w_groups, norm_v.reshape(1, width), lb_logits,
      w_s, b_s.reshape(HEADS_A, GMLP_CHUNK, 1))


def inproj_convert(x, norm1, w_in, norm_v, lb_logits, w_s, b_s, *, span, layer, emit_v, tk):
    m, d = x.shape
    width = w_in.shape[1] // N_GROUPS
    tn = width // HEADS_A
    nt, kt = width // tn, d // tk
    w_spec = lambda g: pl.BlockSpec((tk, tn), lambda i, j, k: (k, g * nt + j))
    tile = pl.BlockSpec((m, tn), lambda i, j, k: (i, j))
    out_dtypes = [BF16] + ([F32] if emit_v else []) + [BF16, F32, BF16, BF16]
    kern = functools.partial(_inproj_convert_kernel, span=span, layer=layer, emit_v=emit_v)
    outs = pl.pallas_call(
        kern,
        grid=(1, nt, kt),
        in_specs=[pl.BlockSpec((m, d), lambda i, j, k: (i, 0)),
                  pl.BlockSpec((1, d), lambda i, j, k: (0, 0))]
                 + [w_spec(g) for g in range(N_GROUPS)]
                 + [pl.BlockSpec((1, tn), lambda i, j, k: (0, j)),
                    pl.BlockSpec((lb_logits.shape[0], tn), lambda i, j, k: (0, j)),
                    pl.BlockSpec((None, GMLP_CHUNK, GMLP_CHUNK), lambda i, j, k: (j, 0, 0)),
                    pl.BlockSpec((None, GMLP_CHUNK, 1), lambda i, j, k: (j, 0, 0))],
        out_specs=[tile] * len(out_dtypes)
                  + [pl.BlockSpec((tk, tn), lambda i, j, k: (k, j))] * N_GROUPS,
        out_shape=[jax.ShapeDtypeStruct((m, width), dt) for dt in out_dtypes]
                  + [jax.ShapeDtypeStruct((d, width), BF16)] * N_GROUPS,
        scratch_shapes=[pltpu.VMEM((kt, m, tk), BF16), pltpu.VMEM((N_GROUPS, m, tn), F32)],
        compiler_params=_params("arbitrary", "arbitrary", "arbitrary"),
        name="inproj_convert",
    )(x, norm1.reshape(1, d), *([w_in] * N_GROUPS), norm_v.reshape(1, width), lb_logits,
      w_s, b_s.reshape(HEADS_A, GMLP_CHUNK, 1))
    return outs[:len(out_dtypes)], outs[len(out_dtypes):]


def _split2_bf16(x):
    hi = x.astype(BF16)
    lo = (x - hi.astype(F32)).astype(BF16)
    return hi, lo


def _hgrn_kernel(q_ref, f_ref, i_ref, g_ref, no_ref, s0_ref, *refs, blk, has_s0):
    n_side = (len(refs) - 3) // 2
    side_in, side_out = refs[:n_side], refs[n_side + 2:-1]
    y_ref, s_ref, st_ref = refs[n_side], refs[n_side + 1], refs[-1]
    tb = pl.program_id(2)
    n_streams, tokens, _ = q_ref.shape
    n_chunks = tokens // blk
    items = [(s, c) for s in range(n_streams) for c in range(n_chunks)]
    rows = lambda c: slice(c * blk, (c + 1) * blk)

    @pl.when(tb == 0)
    def _():
        if has_s0:
            for s in range(n_streams):
                st_ref[s] = s0_ref[s].T
        else:
            st_ref[...] = jnp.zeros_like(st_ref)

    row = lax.broadcasted_iota(jnp.int32, (blk, blk), 0)
    col = lax.broadcasted_iota(jnp.int32, (blk, blk), 1)
    causal = col <= row
    tri = causal.astype(BF16)
    gain = no_ref[...]
    nt = (((1,), (1,)), ((), ()))
    tn = (((0,), (0,)), ((), ()))

    cum = {}
    for s, c in items:
        hi, lo = _split2_bf16(jnp.log(f_ref[s, rows(c), :]))
        cum[s, c] = (jnp.dot(tri, hi, preferred_element_type=F32)
                     + jnp.dot(tri, lo, preferred_element_type=F32))

    q_rel, k_rel, q_abs, k_end, decay, val = {}, {}, {}, {}, {}, {}
    for s, c in items:
        b = cum[s, c]
        b_mid = b[blk // 2:blk // 2 + 1, :]
        b_end = b[blk - 1:blk, :]
        qr = q_ref[s, rows(c), :] * jnp.exp(b - b_mid)
        kr = (1.0 - f_ref[s, rows(c), :]) * jnp.exp(b_mid - b)
        q_rel[s, c] = qr.astype(BF16)
        k_rel[s, c] = kr.astype(BF16)
        q_abs[s, c] = (qr * jnp.exp(b_mid)).astype(BF16)
        k_end[s, c] = (kr * jnp.exp(b_end - b_mid)).astype(BF16)
        decay[s, c] = jnp.exp(b_end)
        val[s, c] = i_ref[s, rows(c), :].astype(BF16)

    scores = {}
    for it in items:
        sc = lax.dot_general(q_rel[it], k_rel[it], nt, preferred_element_type=F32)
        scores[it] = jnp.where(causal, sc, 0.0).astype(BF16)

    out, delta = {}, {}
    for it in items:
        out[it] = jnp.dot(scores[it], val[it], preferred_element_type=F32)
        delta[it] = lax.dot_general(val[it], k_end[it], tn, preferred_element_type=F32)

    for s in range(n_streams):
        st = st_ref[s]
        for c in range(n_chunks):
            out[s, c] = out[s, c] + lax.dot_general(q_abs[s, c], st.astype(BF16), nt,
                                                    preferred_element_type=F32)
            st = st * decay[s, c] + delta[s, c]
        st_ref[s] = st

    for s, c in items:
        o = out[s, c]
        y_ref[s, rows(c), :] = (o * _rms_scale(o) * gain * g_ref[s, rows(c), :]).astype(y_ref.dtype)

    for w_ref, wb_ref in zip(side_in, side_out):
        wb_ref[...] = w_ref[...].astype(wb_ref.dtype)

    @pl.when(tb == pl.num_programs(2) - 1)
    def _():
        for s in range(n_streams):
            s_ref[s] = st_ref[s].T


def hgrn(q, f, i, g, norm_o, s0, *, blk, sb, tb, side_weights=()):
    ns, t, width = q.shape
    heads = width // DK_B
    has_s0 = s0 is not None
    grid = (ns // sb, heads, t // tb)
    n_steps = grid[0] * grid[1] * grid[2]
    tok_spec = pl.BlockSpec((sb, tb, DK_B), lambda b, h, c: (b, c, h))
    state_spec = pl.BlockSpec((sb, None, DK_B, DK_B), lambda b, h, c: (b, h, 0, 0))
    side_specs = []
    for w in side_weights:
        assert w.shape[0] % (n_steps * NORM_ROWS) == 0, "side windows must be whole packed bf16 row tiles"
        side_specs.append(pl.BlockSpec((w.shape[0] // n_steps, w.shape[1]),
                                       lambda b, h, c: ((b * grid[1] + h) * grid[2] + c, 0)))
    if not has_s0:
        s0 = jnp.zeros((ns, heads, DK_B, DK_B), F32)
    kern = functools.partial(_hgrn_kernel, blk=blk, has_s0=has_s0)
    return pl.pallas_call(
        kern,
        grid=grid,
        in_specs=[tok_spec, tok_spec, tok_spec, tok_spec,
                  pl.BlockSpec((None, 1, DK_B), lambda b, h, c: (h, 0, 0)),
                  state_spec] + side_specs,
        out_specs=[tok_spec, state_spec] + side_specs,
        out_shape=[jax.ShapeDtypeStruct((ns, t, width), BF16),
                   jax.ShapeDtypeStruct((ns, heads, DK_B, DK_B), F32)]
                  + [jax.ShapeDtypeStruct(w.shape, BF16) for w in side_weights],
        scratch_shapes=[pltpu.VMEM((sb, DK_B, DK_B), F32)],
        compiler_params=_params("parallel", "parallel", "arbitrary"),
        name="hgrn",
    )(q, f, i, g, norm_o.reshape(heads, 1, DK_B), s0, *side_weights)


def _outproj_kernel(x_ref, ya_ref, yb_ref, wa_ref, wb_ref, o_ref, *wcopy_refs):
    wa, wb = wa_ref[...], wb_ref[...]
    if wcopy_refs:
        wa, wb = wa.astype(BF16), wb.astype(BF16)
        wcopy_refs[0][...] = wa
        wcopy_refs[1][...] = wb
    o_ref[...] = (x_ref[...]
                  + jnp.dot(ya_ref[...], wa, preferred_element_type=F32)
                  + jnp.dot(yb_ref[...], wb, preferred_element_type=F32))


def outproj(x, ya, yb, w_a, w_b, b_block, *, convert, tm, tn):
    m, d = x.shape
    ka = ya.shape[1]
    w_tile = lambda blk: pl.BlockSpec((ka, tn), lambda i, j: (blk, j))
    out_specs = [pl.BlockSpec((tm, tn), lambda i, j: (i, j))]
    out_shape = [jax.ShapeDtypeStruct((m, d), F32)]
    if convert:
        assert m == tm, "weight copies are written once per row block"
        out_specs += [w_tile(0)] * 2
        out_shape += [jax.ShapeDtypeStruct((ka, d), BF16)] * 2
    return pl.pallas_call(
        _outproj_kernel,
        grid=(m // tm, d // tn),
        in_specs=[pl.BlockSpec((tm, tn), lambda i, j: (i, j)),
                  pl.BlockSpec((tm, ka), lambda i, j: (i, 0)),
                  pl.BlockSpec((tm, ka), lambda i, j: (i, 0)),
                  w_tile(0), w_tile(b_block)],
        out_specs=out_specs,
        out_shape=out_shape,
        compiler_params=_params("parallel", "arbitrary"),
        name="outproj_convert" if convert else "outproj",
    )(x, ya, yb, w_a, w_b)


def _mlp_kernel(x_hbm, n2_ref, wu_ref, wd_ref, nf_ref, o_ref, wub_ref, h_ref, sem, *, final_norm):
    i, f = pl.program_id(0), pl.program_id(1)
    tm = o_ref.shape[0]

    @pl.when(f == 0)
    def _():
        load = pltpu.make_async_copy(x_hbm.at[pl.ds(pl.multiple_of(i * tm, tm), tm), :], o_ref, sem)
        load.start()
        load.wait()

        def norm_rows(rs):
            x = o_ref[rs, :]
            h_ref[rs, :] = (x * _rms_scale(x) * n2_ref[...]).astype(h_ref.dtype)
        _for_row_chunks(tm, norm_rows)

    wu = wu_ref[...].astype(BF16)
    wub_ref[...] = wu
    hid = jnp.dot(h_ref[...], wu, preferred_element_type=F32)
    hid = jnp.square(jnp.maximum(hid, 0.0)).astype(BF16)
    for n in range(o_ref.shape[1] // MLP_ACC_COLS):
        sl = slice(n * MLP_ACC_COLS, (n + 1) * MLP_ACC_COLS)
        o_ref[:, sl] += jnp.dot(hid, wd_ref[:, sl], preferred_element_type=F32)

    if final_norm:
        @pl.when(f == pl.num_programs(1) - 1)
        def _():
            def norm_rows(rs):
                x = o_ref[rs, :]
                o_ref[rs, :] = x * _rms_scale(x) * nf_ref[...]
            _for_row_chunks(o_ref.shape[0], norm_rows)


def mlp_convert_up(x, norm2, w_up, w_down, norm_f, *, final_norm, tf):
    m, d = x.shape
    ff = w_up.shape[1]
    once = pl.Buffered(1)
    up_tile = pl.BlockSpec((d, tf), lambda i, f: (0, f))
    kern = functools.partial(_mlp_kernel, final_norm=final_norm)
    return pl.pallas_call(
        kern,
        grid=(1, ff // tf),
        in_specs=[pl.BlockSpec(memory_space=pl.ANY),
                  pl.BlockSpec((1, d), lambda i, f: (0, 0), pipeline_mode=once),
                  up_tile,
                  pl.BlockSpec((tf, d), lambda i, f: (f, 0)),
                  pl.BlockSpec((1, d), lambda i, f: (0, 0), pipeline_mode=once)],
        out_specs=[pl.BlockSpec((m, d), lambda i, f: (i, 0)), up_tile],
        out_shape=[jax.ShapeDtypeStruct((m, d), F32), jax.ShapeDtypeStruct((d, ff), BF16)],
        scratch_shapes=[pltpu.VMEM((m, d), BF16), pltpu.SemaphoreType.DMA(())],
        compiler_params=_params("arbitrary", "arbitrary"),
        name="mlp_convert_up",
    )(x, norm2.reshape(1, d), w_up, w_down, norm_f.reshape(1, d))


def _mlp_stream_kernel(x_hbm, xn_ref, xc_ref, n2_ref, wu_ref, wd_ref, nf_ref, o_ref, h_ref, warm_ref, sem,
                       *, final_norm):
    i, f = pl.program_id(0), pl.program_id(1)
    tm, d = o_ref.shape
    step_rows = xn_ref.shape[0]
    slot = i % 2
    norm = lambda x: (x * _rms_scale(x) * n2_ref[...]).astype(h_ref.dtype)

    @pl.when(jnp.logical_and(i == 0, f == 0))
    def _():
        def store_normed(rs, x):
            h_ref[0, rs, :] = norm(x)
        _warm_up_rows(x_hbm, warm_ref, sem, tm, store_normed)

    def step(first):
        rs = pl.ds(pl.multiple_of(f * step_rows, step_rows), step_rows)
        if not first:
            o_ref[rs, :] += xc_ref[...]
        hid = jnp.dot(h_ref[slot], wu_ref[...], preferred_element_type=F32)
        hid = jnp.square(jnp.maximum(hid, 0.0)).astype(BF16)
        for n in range(d // MLP_ACC_COLS):
            sl = slice(n * MLP_ACC_COLS, (n + 1) * MLP_ACC_COLS)
            part = jnp.dot(hid, wd_ref[:, sl], preferred_element_type=F32)
            if first:
                o_ref[:, sl] = part
            else:
                o_ref[:, sl] += part
        if first:
            o_ref[rs, :] += xc_ref[...]
        h_ref[1 - slot, rs, :] = norm(xn_ref[...])

    pl.when(f == 0)(lambda: step(True))
    pl.when(f > 0)(lambda: step(False))

    if final_norm:
        @pl.when(f == pl.num_programs(1) - 1)
        def _():
            def norm_rows(rs):
                x = o_ref[rs, :]
                o_ref[rs, :] = x * _rms_scale(x) * nf_ref[...]
            _for_row_chunks(tm, norm_rows, chunk=F32_SUBLANES)


def mlp_stream(x, norm2, w_up, w_down, norm_f, *, final_norm, tm, tf):
    m, d = x.shape
    ff = w_up.shape[1]
    n_row_blocks, n_f = m // tm, ff // tf
    step_rows = tm // n_f
    once = pl.Buffered(1)
    kern = functools.partial(_mlp_stream_kernel, final_norm=final_norm)
    return pl.pallas_call(
        kern,
        grid=(n_row_blocks, n_f),
        in_specs=[pl.BlockSpec(memory_space=pl.ANY),
                  pl.BlockSpec((step_rows, d), lambda i, f: (jnp.minimum(i + 1, n_row_blocks - 1) * n_f + f, 0)),
                  pl.BlockSpec((step_rows, d), lambda i, f: (i * n_f + f, 0)),
                  pl.BlockSpec((1, d), lambda i, f: (0, 0), pipeline_mode=once),
                  pl.BlockSpec((d, tf), lambda i, f: (0, f)),
                  pl.BlockSpec((tf, d), lambda i, f: (f, 0)),
                  pl.BlockSpec((1, d), lambda i, f: (0, 0), pipeline_mode=once)],
        out_specs=pl.BlockSpec((tm, d), lambda i, f: (i, 0), pipeline_mode=once),
        out_shape=jax.ShapeDtypeStruct((m, d), F32),
        scratch_shapes=[pltpu.VMEM((2, tm, d), BF16), pltpu.VMEM((MLP_WARM_ROWS, d), F32),
                        pltpu.SemaphoreType.DMA(())],
        compiler_params=_params("arbitrary", "arbitrary"),
        name="mlp",
    )(x, x, x, norm2.reshape(1, d), w_up, w_down, norm_f.reshape(1, d))


def _mix_heads(x3, s0, lb_logits, p, *, layer, emit_v, convert, side_weights=()):
    ns, t, d = x3.shape
    m = ns * t
    x = x3.reshape(m, d)
    span = min(t, GMLP_CHUNK)
    small = dict(norm_v=p["norm_v"], lb_logits=lb_logits, w_s=p["w_s"], b_s=p["b_s"])
    w_in_bf = None
    if convert:
        outs, w_in_bf = inproj_convert(x, p["norm1"], p["w_in"], **small, span=span, layer=layer,
                                       emit_v=emit_v, tk=1024)
    else:
        outs = inproj(x, p["norm1"], p["w_in"], **small, span=span, layer=layer, emit_v=emit_v,
                      tm=min(m, 1024))
    ya, (q, f, i, g) = outs[0], outs[-4:]
    width = ya.shape[1]
    if t >= 2048:
        sb, tb = 1, 2048
    else:
        sb, tb = ns, t
    per_stream = lambda a: a.reshape(ns, t, width)
    yb3, s_new, *side_bf = hgrn(per_stream(q), per_stream(f), per_stream(i), per_stream(g), p["norm_o"], s0,
                                blk=min(t, CHUNK), sb=sb, tb=tb, side_weights=side_weights)
    v_act = per_stream(outs[1]) if emit_v else None
    return x, ya, yb3.reshape(m, width), s_new, v_act, w_in_bf, side_bf


def kernel(x_prompt, x_sample, state_hgrn, norm1, w_in, w_s, b_s, norm_v, lb_logits, norm_o,
           w_out, norm2, w_up, w_down, norm_f):
    depth = w_in.shape[0]
    lb_logits = lb_logits.astype(F32)
    xp, xs = x_prompt, x_sample
    s_prompt, s_sample, v_sample = [], [], []
    for l in range(depth):
        p = dict(norm1=norm1[l], w_in=w_in[l], w_s=w_s[l], b_s=b_s[l], norm_v=norm_v[l], norm_o=norm_o[l])
        final_norm = l == depth - 1
        rows_s, ya_s, yb_s, ss, vs, w_in_bf, _ = _mix_heads(xs, state_hgrn[l], lb_logits, p, layer=l,
                                                             emit_v=True, convert=True)
        rows_p, ya_p, yb_p, sp, _, _, (w_down_bf,) = _mix_heads(
            xp, None, lb_logits, {**p, "w_in": w_in_bf}, layer=l, emit_v=False, convert=False,
            side_weights=(w_down[l],))
        x1_s, *w_out_bf = outproj(rows_s, ya_s, yb_s, w_out[l], w_out[l], 1, convert=True,
                                  tm=rows_s.shape[0], tn=512)
        x2_s, w_up_bf = mlp_convert_up(x1_s, norm2[l], w_up[l], w_down_bf, norm_f, final_norm=final_norm, tf=512)
        x1_p, = outproj(rows_p, ya_p, yb_p, *w_out_bf, 0, convert=False, tm=1024, tn=1024)
        x2_p = mlp_stream(x1_p, norm2[l], w_up_bf, w_down_bf, norm_f, final_norm=final_norm, tm=1024, tf=512)
        xs, xp = x2_s.reshape(xs.shape), x2_p.reshape(xp.shape)
        s_prompt.append(sp)
        s_sample.append(ss)
        v_sample.append(vs)
    return (xp, xs, jnp.stack(s_prompt), jnp.stack(s_sample), jnp.stack(v_sample))
```

```python
import functools

import jax
import jax.numpy as jnp
from jax import lax
from jax.experimental import pallas as pl
from jax.experimental.pallas import tpu as pltpu

EPS = 1e-6
CHUNK = 64
GMLP_CHUNK = 128
HEADS_A = 8
N_GROUPS = 6
DK_B = 128
MLP_ACC_COLS = 1024
NORM_ROWS = 16
F32_SUBLANES = 8
MLP_WARM_ROWS = 128
BF16 = jnp.bfloat16
F32 = jnp.float32

V7X_VMEM_BYTES = 64 * 1024 * 1024
VMEM_LIMIT_BYTES = V7X_VMEM_BYTES - 4 * 1024 * 1024

ROW_BLOCK = 1024
INPROJ_CONVERT_TK = 1024
OUTPROJ_TILE = (2048, 512)
OUTPROJ_CONVERT_TN = 512
MLP_TF = 512
HGRN_LONG_BLOCK = (4, 512)


def _params(*semantics):
    return pltpu.CompilerParams(dimension_semantics=semantics, vmem_limit_bytes=VMEM_LIMIT_BYTES)


def _rms_scale(x):
    return lax.rsqrt(jnp.mean(x * x, axis=-1, keepdims=True) + EPS)


def _for_row_chunks(n_rows, body, chunk=NORM_ROWS):
    for r in range(0, n_rows, chunk):
        body(slice(r, r + chunk))


def _inproj_epilogue(pre, nv_ref, lb_ref, ws_ref, bs_ref, out_refs, *, span, layer, emit_v):
    if emit_v:
        ya_ref, v_ref, q_ref, f_ref, i_ref, g_ref = out_refs
    else:
        ya_ref, q_ref, f_ref, i_ref, g_ref = out_refs

    u = jax.nn.gelu(pre(0))
    a = jax.nn.gelu(pre(1))
    v = a * _rms_scale(a) * nv_ref[...]
    if emit_v:
        v_ref[...] = v
    row = lax.broadcasted_iota(jnp.int32, (span, span), 0)
    col = lax.broadcasted_iota(jnp.int32, (span, span), 1)
    w = jnp.where(col // CHUNK <= row // CHUNK, ws_ref[:span, :span], 0.0).astype(BF16)
    bias = bs_ref[:span, :]
    for c in range(u.shape[0] // span):
        sl = slice(c * span, (c + 1) * span)
        mixed = jnp.dot(w, v[sl, :].astype(BF16), preferred_element_type=F32) + bias
        ya_ref[sl, :] = (u[sl, :] * mixed).astype(ya_ref.dtype)

    qz = pre(2)
    q_ref[...] = (qz * jax.nn.sigmoid(qz)).astype(q_ref.dtype)

    lg = lb_ref[...]
    e = jnp.exp(lg - jnp.max(lg, axis=0, keepdims=True))
    sm = e / jnp.sum(e, axis=0, keepdims=True)
    lb = jnp.sum(sm[:layer + 1, :], axis=0, keepdims=True)
    f_ref[...] = lb + (1.0 - lb) * jax.nn.sigmoid(pre(3))

    i_ref[...] = pre(4).astype(i_ref.dtype)

    gz = pre(5)
    g_ref[...] = (gz * jax.nn.sigmoid(gz)).astype(g_ref.dtype)


def _warm_up_rows(x_hbm, warm_ref, sem, n_rows, store_normed):
    warm_rows = warm_ref.shape[0]
    for c in range(n_rows // warm_rows):
        load = pltpu.make_async_copy(x_hbm.at[pl.ds(c * warm_rows, warm_rows), :], warm_ref, sem)
        load.start()
        load.wait()
        for r in range(0, warm_rows, NORM_ROWS):
            store_normed(slice(c * warm_rows + r, c * warm_rows + r + NORM_ROWS), warm_ref[r:r + NORM_ROWS, :])


def _inproj_kernel(x_hbm, n1_ref, *refs, span, layer, emit_v):
    w_refs = refs[:N_GROUPS]
    nv_ref, lb_ref, ws_ref, bs_ref = refs[N_GROUPS:N_GROUPS + 4]
    out_refs = refs[N_GROUPS + 4:-3]
    h_ref, x_buf, sem = refs[-3:]
    i, j = pl.program_id(0), pl.program_id(1)
    tm = h_ref.shape[0]

    def x_rows_copy(block):
        rows = pl.ds(pl.multiple_of(block * tm, tm), tm)
        return pltpu.make_async_copy(x_hbm.at[rows, :], x_buf, sem)

    @pl.when(j == 0)
    def _():
        @pl.when(i == 0)
        def _():
            x_rows_copy(0).start()
        x_rows_copy(i).wait()

        def norm_rows(rs):
            x = x_buf[rs, :]
            h_ref[rs, :] = (x * _rms_scale(x) * n1_ref[...]).astype(h_ref.dtype)
        _for_row_chunks(tm, norm_rows)

    @pl.when(jnp.logical_and(j == 1, i + 1 < pl.num_programs(0)))
    def _():
        x_rows_copy(i + 1).start()

    pre = lambda g: jnp.dot(h_ref[...], w_refs[g][...], preferred_element_type=F32)
    _inproj_epilogue(pre, nv_ref, lb_ref, ws_ref, bs_ref, out_refs, span=span, layer=layer, emit_v=emit_v)


def _inproj_convert_kernel(x_ref, n1_ref, *refs, span, layer, emit_v):
    w_refs = refs[:N_GROUPS]
    nv_ref, lb_ref, ws_ref, bs_ref = refs[N_GROUPS:N_GROUPS + 4]
    n_act = 6 if emit_v else 5
    out_refs = refs[N_GROUPS + 4:N_GROUPS + 4 + n_act]
    wb_refs = refs[N_GROUPS + 4 + n_act:-2]
    h_ref, acc_ref = refs[-2:]
    j, k = pl.program_id(1), pl.program_id(2)
    kt, _, tk = h_ref.shape

    @pl.when(jnp.logical_and(j == 0, k == 0))
    def _():
        def norm_rows(rs):
            x = x_ref[rs, :]
            h = (x * _rms_scale(x) * n1_ref[...]).astype(h_ref.dtype)
            for kk in range(kt):
                h_ref[kk, rs, :] = h[:, kk * tk:(kk + 1) * tk]
        _for_row_chunks(x_ref.shape[0], norm_rows)

    @pl.when(k == 0)
    def _():
        acc_ref[...] = jnp.zeros_like(acc_ref)

    for g in range(N_GROUPS):
        wb = w_refs[g][...].astype(BF16)
        wb_refs[g][...] = wb
        acc_ref[g] += jnp.dot(h_ref[k], wb, preferred_element_type=F32)

    @pl.when(k == kt - 1)
    def _():
        _inproj_epilogue(lambda g: acc_ref[g], nv_ref, lb_ref, ws_ref, bs_ref, out_refs,
                         span=span, layer=layer, emit_v=emit_v)


def inproj(x, norm1, w_groups, norm_v, lb_logits, w_s, b_s, *, span, layer, emit_v, tm):
    m, d = x.shape
    width = w_groups[0].shape[1]
    tn = width // HEADS_A
    tile = pl.BlockSpec((tm, tn), lambda i, j: (i, j))
    out_dtypes = [BF16] + ([F32] if emit_v else []) + [BF16, F32, BF16, BF16]
    kern = functools.partial(_inproj_kernel, span=span, layer=layer, emit_v=emit_v)
    return pl.pallas_call(
        kern,
        grid=(m // tm, width // tn),
        in_specs=[pl.BlockSpec(memory_space=pl.ANY),
                  pl.BlockSpec((1, d), lambda i, j: (0, 0))]
                 + [pl.BlockSpec((d, tn), lambda i, j: (0, j))] * N_GROUPS
                 + [pl.BlockSpec((1, tn), lambda i, j: (0, j)),
                    pl.BlockSpec((lb_logits.shape[0], tn), lambda i, j: (0, j)),
                    pl.BlockSpec((None, GMLP_CHUNK, GMLP_CHUNK), lambda i, j: (j, 0, 0)),
                    pl.BlockSpec((None, GMLP_CHUNK, 1), lambda i, j: (j, 0, 0))],
        out_specs=[tile] * len(out_dtypes),
        out_shape=[jax.ShapeDtypeStruct((m, width), dt) for dt in out_dtypes],
        scratch_shapes=[pltpu.VMEM((tm, d), BF16), pltpu.VMEM((tm, d), F32), pltpu.SemaphoreType.DMA(())],
        compiler_params=_params("arbitrary", "arbitrary"),
        name="inproj",
    )(x, norm1.reshape(1, d), *w_groups, norm_v.reshape(1, width), lb_logits,
      w_s, b_s.reshape(HEADS_A, GMLP_CHUNK, 1))


def inproj_convert(x, norm1, w_in, norm_v, lb_logits, w_s, b_s, *, span, layer, emit_v, tk):
    m, d = x.shape
    width = w_in.shape[1] // N_GROUPS
    tn = width // HEADS_A
    nt, kt = width // tn, d // tk
    w_spec = lambda g: pl.BlockSpec((tk, tn), lambda i, j, k: (k, g * nt + j))
    tile = pl.BlockSpec((m, tn), lambda i, j, k: (i, j))
    out_dtypes = [BF16] + ([F32] if emit_v else []) + [BF16, F32, BF16, BF16]
    kern = functools.partial(_inproj_convert_kernel, span=span, layer=layer, emit_v=emit_v)
    outs = pl.pallas_call(
        kern,
        grid=(1, nt, kt),
        in_specs=[pl.BlockSpec((m, d), lambda i, j, k: (i, 0)),
                  pl.BlockSpec((1, d), lambda i, j, k: (0, 0))]
                 + [w_spec(g) for g in range(N_GROUPS)]
                 + [pl.BlockSpec((1, tn), lambda i, j, k: (0, j)),
                    pl.BlockSpec((lb_logits.shape[0], tn), lambda i, j, k: (0, j)),
                    pl.BlockSpec((None, GMLP_CHUNK, GMLP_CHUNK), lambda i, j, k: (j, 0, 0)),
                    pl.BlockSpec((None, GMLP_CHUNK, 1), lambda i, j, k: (j, 0, 0))],
        out_specs=[tile] * len(out_dtypes)
                  + [pl.BlockSpec((tk, tn), lambda i, j, k: (k, j))] * N_GROUPS,
        out_shape=[jax.ShapeDtypeStruct((m, width), dt) for dt in out_dtypes]
                  + [jax.ShapeDtypeStruct((d, width), BF16)] * N_GROUPS,
        scratch_shapes=[pltpu.VMEM((kt, m, tk), BF16), pltpu.VMEM((N_GROUPS, m, tn), F32)],
        compiler_params=_params("arbitrary", "arbitrary", "arbitrary"),
        name="inproj_convert",
    )(x, norm1.reshape(1, d), *([w_in] * N_GROUPS), norm_v.reshape(1, width), lb_logits,
      w_s, b_s.reshape(HEADS_A, GMLP_CHUNK, 1))
    return outs[:len(out_dtypes)], outs[len(out_dtypes):]


def _split2_bf16(x):
    hi = x.astype(BF16)
    lo = (x - hi.astype(F32)).astype(BF16)
    return hi, lo


def _hgrn_kernel(q_ref, f_ref, i_ref, g_ref, no_ref, s0_ref, *refs, blk, has_s0):
    n_side = (len(refs) - 3) // 2
    side_in, side_out = refs[:n_side], refs[n_side + 2:-1]
    y_ref, s_ref, st_ref = refs[n_side], refs[n_side + 1], refs[-1]
    tb = pl.program_id(2)
    n_streams, tokens, lanes = q_ref.shape
    n_heads, n_chunks = lanes // DK_B, tokens // blk
    chains = [(s, h) for s in range(n_streams) for h in range(n_heads)]
    items = [(s, h, c) for c in range(n_chunks) for s, h in chains]
    tile = lambda ref, s, h, c: ref[s, c * blk:(c + 1) * blk, h * DK_B:(h + 1) * DK_B]

    @pl.when(tb == 0)
    def _():
        for s, h in chains:
            st_ref[s, h] = s0_ref[s, h].T if has_s0 else jnp.zeros((DK_B, DK_B), F32)

    row = lax.broadcasted_iota(jnp.int32, (blk, blk), 0)
    col = lax.broadcasted_iota(jnp.int32, (blk, blk), 1)
    causal = col <= row
    tri = causal.astype(BF16)
    nt = (((1,), (1,)), ((), ()))
    tn = (((0,), (0,)), ((), ()))

    cum = {}
    for it in items:
        hi, lo = _split2_bf16(jnp.log(tile(f_ref, *it)))
        cum[it] = (jnp.dot(tri, hi, preferred_element_type=F32)
                   + jnp.dot(tri, lo, preferred_element_type=F32))

    q_rel, k_rel, q_abs, k_end, decay, val = {}, {}, {}, {}, {}, {}
    for it in items:
        b = cum[it]
        b_mid = b[blk // 2:blk // 2 + 1, :]
        b_end = b[blk - 1:blk, :]
        qr = tile(q_ref, *it) * jnp.exp(b - b_mid)
        kr = (1.0 - tile(f_ref, *it)) * jnp.exp(b_mid - b)
        q_rel[it] = qr.astype(BF16)
        k_rel[it] = kr.astype(BF16)
        q_abs[it] = (qr * jnp.exp(b_mid)).astype(BF16)
        k_end[it] = (kr * jnp.exp(b_end - b_mid)).astype(BF16)
        decay[it] = jnp.exp(b_end)
        val[it] = tile(i_ref, *it).astype(BF16)

    scores = {}
    for it in items:
        sc = lax.dot_general(q_rel[it], k_rel[it], nt, preferred_element_type=F32)
        scores[it] = jnp.where(causal, sc, 0.0).astype(BF16)

    out, delta = {}, {}
    for it in items:
        out[it] = jnp.dot(scores[it], val[it], preferred_element_type=F32)
        delta[it] = lax.dot_general(val[it], k_end[it], tn, preferred_element_type=F32)

    state = {ch: st_ref[ch] for ch in chains}
    for s, h, c in items:
        st = state[s, h]
        out[s, h, c] = out[s, h, c] + lax.dot_general(q_abs[s, h, c], st.astype(BF16), nt,
                                                      preferred_element_type=F32)
        state[s, h] = st * decay[s, h, c] + delta[s, h, c]
    for ch in chains:
        st_ref[ch] = state[ch]

    for s, h, c in items:
        o = out[s, h, c]
        y = o * _rms_scale(o) * no_ref[h] * tile(g_ref, s, h, c)
        y_ref[s, c * blk:(c + 1) * blk, h * DK_B:(h + 1) * DK_B] = y.astype(y_ref.dtype)

    for w_ref, wb_ref in zip(side_in, side_out):
        wb_ref[...] = w_ref[...].astype(wb_ref.dtype)

    @pl.when(tb == pl.num_programs(2) - 1)
    def _():
        for ch in chains:
            s_ref[ch] = st_ref[ch].T


def hgrn(q, f, i, g, norm_o, s0, *, blk, sb, hb, tb, side_weights=()):
    ns, t, width = q.shape
    heads = width // DK_B
    has_s0 = s0 is not None
    grid = (ns // sb, heads // hb, t // tb)
    n_steps = grid[0] * grid[1] * grid[2]
    tok_spec = pl.BlockSpec((sb, tb, hb * DK_B), lambda b, h, c: (b, c, h))
    state_spec = pl.BlockSpec((sb, hb, DK_B, DK_B), lambda b, h, c: (b, h, 0, 0))
    side_specs = []
    for w in side_weights:
        assert w.shape[0] % (n_steps * NORM_ROWS) == 0, "side windows must be whole packed bf16 row tiles"
        side_specs.append(pl.BlockSpec((w.shape[0] // n_steps, w.shape[1]),
                                       lambda b, h, c: ((b * grid[1] + h) * grid[2] + c, 0)))
    if not has_s0:
        s0 = jnp.zeros((ns, heads, DK_B, DK_B), F32)
    kern = functools.partial(_hgrn_kernel, blk=blk, has_s0=has_s0)
    return pl.pallas_call(
        kern,
        grid=grid,
        in_specs=[tok_spec, tok_spec, tok_spec, tok_spec,
                  pl.BlockSpec((hb, 1, DK_B), lambda b, h, c: (h, 0, 0)),
                  state_spec] + side_specs,
        out_specs=[tok_spec, state_spec] + side_specs,
        out_shape=[jax.ShapeDtypeStruct((ns, t, width), BF16),
                   jax.ShapeDtypeStruct((ns, heads, DK_B, DK_B), F32)]
                  + [jax.ShapeDtypeStruct(w.shape, BF16) for w in side_weights],
        scratch_shapes=[pltpu.VMEM((sb, hb, DK_B, DK_B), F32)],
        compiler_params=_params("parallel", "parallel", "arbitrary"),
        name="hgrn",
    )(q, f, i, g, norm_o.reshape(heads, 1, DK_B), s0, *side_weights)


def _outproj_kernel(x_ref, ya_ref, yb_ref, wa_ref, wb_ref, o_ref, *wcopy_refs):
    wa, wb = wa_ref[...], wb_ref[...]
    if wcopy_refs:
        wa, wb = wa.astype(BF16), wb.astype(BF16)
        wcopy_refs[0][...] = wa
        wcopy_refs[1][...] = wb
    o_ref[...] = (x_ref[...]
                  + jnp.dot(ya_ref[...], wa, preferred_element_type=F32)
                  + jnp.dot(yb_ref[...], wb, preferred_element_type=F32))


def outproj(x, ya, yb, w_a, w_b, b_block, *, convert, tm, tn):
    m, d = x.shape
    ka = ya.shape[1]
    w_tile = lambda blk: pl.BlockSpec((ka, tn), lambda i, j: (blk, j))
    out_specs = [pl.BlockSpec((tm, tn), lambda i, j: (i, j))]
    out_shape = [jax.ShapeDtypeStruct((m, d), F32)]
    if convert:
        assert m == tm, "weight copies are written once per row block"
        out_specs += [w_tile(0)] * 2
        out_shape += [jax.ShapeDtypeStruct((ka, d), BF16)] * 2
    return pl.pallas_call(
        _outproj_kernel,
        grid=(m // tm, d // tn),
        in_specs=[pl.BlockSpec((tm, tn), lambda i, j: (i, j)),
                  pl.BlockSpec((tm, ka), lambda i, j: (i, 0)),
                  pl.BlockSpec((tm, ka), lambda i, j: (i, 0)),
                  w_tile(0), w_tile(b_block)],
        out_specs=out_specs,
        out_shape=out_shape,
        compiler_params=_params("parallel", "arbitrary"),
        name="outproj_convert" if convert else "outproj",
    )(x, ya, yb, w_a, w_b)


def _mlp_kernel(x_hbm, n2_ref, wu_ref, wd_ref, nf_ref, o_ref, wub_ref, h_ref, sem, *, final_norm):
    i, f = pl.program_id(0), pl.program_id(1)
    tm = o_ref.shape[0]

    @pl.when(f == 0)
    def _():
        load = pltpu.make_async_copy(x_hbm.at[pl.ds(pl.multiple_of(i * tm, tm), tm), :], o_ref, sem)
        load.start()
        load.wait()

        def norm_rows(rs):
            x = o_ref[rs, :]
            h_ref[rs, :] = (x * _rms_scale(x) * n2_ref[...]).astype(h_ref.dtype)
        _for_row_chunks(tm, norm_rows)

    wu = wu_ref[...].astype(BF16)
    wub_ref[...] = wu
    hid = jnp.dot(h_ref[...], wu, preferred_element_type=F32)
    hid = jnp.square(jnp.maximum(hid, 0.0)).astype(BF16)
    for n in range(o_ref.shape[1] // MLP_ACC_COLS):
        sl = slice(n * MLP_ACC_COLS, (n + 1) * MLP_ACC_COLS)
        o_ref[:, sl] += jnp.dot(hid, wd_ref[:, sl], preferred_element_type=F32)

    if final_norm:
        @pl.when(f == pl.num_programs(1) - 1)
        def _():
            def norm_rows(rs):
                x = o_ref[rs, :]
                o_ref[rs, :] = x * _rms_scale(x) * nf_ref[...]
            _for_row_chunks(o_ref.shape[0], norm_rows)


def mlp_convert_up(x, norm2, w_up, w_down, norm_f, *, final_norm, tf):
    m, d = x.shape
    ff = w_up.shape[1]
    once = pl.Buffered(1)
    up_tile = pl.BlockSpec((d, tf), lambda i, f: (0, f))
    kern = functools.partial(_mlp_kernel, final_norm=final_norm)
    return pl.pallas_call(
        kern,
        grid=(1, ff // tf),
        in_specs=[pl.BlockSpec(memory_space=pl.ANY),
                  pl.BlockSpec((1, d), lambda i, f: (0, 0), pipeline_mode=once),
                  up_tile,
                  pl.BlockSpec((tf, d), lambda i, f: (f, 0)),
                  pl.BlockSpec((1, d), lambda i, f: (0, 0), pipeline_mode=once)],
        out_specs=[pl.BlockSpec((m, d), lambda i, f: (i, 0)), up_tile],
        out_shape=[jax.ShapeDtypeStruct((m, d), F32), jax.ShapeDtypeStruct((d, ff), BF16)],
        scratch_shapes=[pltpu.VMEM((m, d), BF16), pltpu.SemaphoreType.DMA(())],
        compiler_params=_params("arbitrary", "arbitrary"),
        name="mlp_convert_up",
    )(x, norm2.reshape(1, d), w_up, w_down, norm_f.reshape(1, d))


def _mlp_stream_kernel(x_hbm, xn_ref, xc_ref, n2_ref, wu_ref, wd_ref, nf_ref, o_ref, h_ref, warm_ref, sem,
                       *, final_norm):
    i, f = pl.program_id(0), pl.program_id(1)
    tm, d = o_ref.shape
    step_rows = xn_ref.shape[0]
    slot = i % 2
    norm = lambda x: (x * _rms_scale(x) * n2_ref[...]).astype(h_ref.dtype)

    @pl.when(jnp.logical_and(i == 0, f == 0))
    def _():
        def store_normed(rs, x):
            h_ref[0, rs, :] = norm(x)
        _warm_up_rows(x_hbm, warm_ref, sem, tm, store_normed)

    def step(first):
        rs = pl.ds(pl.multiple_of(f * step_rows, step_rows), step_rows)
        if not first:
            o_ref[rs, :] += xc_ref[...]
        hid = jnp.dot(h_ref[slot], wu_ref[...], preferred_element_type=F32)
        hid = jnp.square(jnp.maximum(hid, 0.0)).astype(BF16)
        for n in range(d // MLP_ACC_COLS):
            sl = slice(n * MLP_ACC_COLS, (n + 1) * MLP_ACC_COLS)
            part = jnp.dot(hid, wd_ref[:, sl], preferred_element_type=F32)
            if first:
                o_ref[:, sl] = part
            else:
                o_ref[:, sl] += part
        if first:
            o_ref[rs, :] += xc_ref[...]
        h_ref[1 - slot, rs, :] = norm(xn_ref[...])

    pl.when(f == 0)(lambda: step(True))
    pl.when(f > 0)(lambda: step(False))

    if final_norm:
        @pl.when(f == pl.num_programs(1) - 1)
        def _():
            def norm_rows(rs):
                x = o_ref[rs, :]
                o_ref[rs, :] = x * _rms_scale(x) * nf_ref[...]
            _for_row_chunks(tm, norm_rows, chunk=F32_SUBLANES)


def mlp_stream(x, norm2, w_up, w_down, norm_f, *, final_norm, tm, tf):
    m, d = x.shape
    ff = w_up.shape[1]
    n_row_blocks, n_f = m // tm, ff // tf
    step_rows = tm // n_f
    once = pl.Buffered(1)
    kern = functools.partial(_mlp_stream_kernel, final_norm=final_norm)
    return pl.pallas_call(
        kern,
        grid=(n_row_blocks, n_f),
        in_specs=[pl.BlockSpec(memory_space=pl.ANY),
                  pl.BlockSpec((step_rows, d), lambda i, f: (jnp.minimum(i + 1, n_row_blocks - 1) * n_f + f, 0)),
                  pl.BlockSpec((step_rows, d), lambda i, f: (i * n_f + f, 0)),
                  pl.BlockSpec((1, d), lambda i, f: (0, 0), pipeline_mode=once),
                  pl.BlockSpec((d, tf), lambda i, f: (0, f)),
                  pl.BlockSpec((tf, d), lambda i, f: (f, 0)),
                  pl.BlockSpec((1, d), lambda i, f: (0, 0), pipeline_mode=once)],
        out_specs=pl.BlockSpec((tm, d), lambda i, f: (i, 0), pipeline_mode=once),
        out_shape=jax.ShapeDtypeStruct((m, d), F32),
        scratch_shapes=[pltpu.VMEM((2, tm, d), BF16), pltpu.VMEM((MLP_WARM_ROWS, d), F32),
                        pltpu.SemaphoreType.DMA(())],
        compiler_params=_params("arbitrary", "arbitrary"),
        name="mlp",
    )(x, x, x, norm2.reshape(1, d), w_up, w_down, norm_f.reshape(1, d))


def _mix_heads(x3, s0, lb_logits, p, *, layer, emit_v, convert, side_weights=()):
    ns, t, d = x3.shape
    m = ns * t
    x = x3.reshape(m, d)
    span = min(t, GMLP_CHUNK)
    small = dict(norm_v=p["norm_v"], lb_logits=lb_logits, w_s=p["w_s"], b_s=p["b_s"])
    w_in_bf = None
    if convert:
        outs, w_in_bf = inproj_convert(x, p["norm1"], p["w_in"], **small, span=span, layer=layer,
                                       emit_v=emit_v, tk=INPROJ_CONVERT_TK)
    else:
        outs = inproj(x, p["norm1"], p["w_in"], **small, span=span, layer=layer, emit_v=emit_v,
                      tm=min(m, ROW_BLOCK))
    ya, (q, f, i, g) = outs[0], outs[-4:]
    width = ya.shape[1]
    if t >= HGRN_LONG_BLOCK[1]:
        sb, (hb, tb) = 1, HGRN_LONG_BLOCK
    else:
        sb, hb, tb = ns, 1, t
    per_stream = lambda a: a.reshape(ns, t, width)
    yb3, s_new, *side_bf = hgrn(per_stream(q), per_stream(f), per_stream(i), per_stream(g), p["norm_o"], s0,
                                blk=min(t, CHUNK), sb=sb, hb=hb, tb=tb, side_weights=side_weights)
    v_act = per_stream(outs[1]) if emit_v else None
    return x, ya, yb3.reshape(m, width), s_new, v_act, w_in_bf, side_bf


def kernel(x_prompt, x_sample, state_hgrn, norm1, w_in, w_s, b_s, norm_v, lb_logits, norm_o,
           w_out, norm2, w_up, w_down, norm_f):
    depth = w_in.shape[0]
    lb_logits = lb_logits.astype(F32)
    xp, xs = x_prompt, x_sample
    s_prompt, s_sample, v_sample = [], [], []
    for l in range(depth):
        p = dict(norm1=norm1[l], w_in=w_in[l], w_s=w_s[l], b_s=b_s[l], norm_v=norm_v[l], norm_o=norm_o[l])
        final_norm = l == depth - 1
        rows_s, ya_s, yb_s, ss, vs, w_in_bf, _ = _mix_heads(xs, state_hgrn[l], lb_logits, p, layer=l,
                                                             emit_v=True, convert=True)
        rows_p, ya_p, yb_p, sp, _, _, (w_down_bf,) = _mix_heads(
            xp, None, lb_logits, {**p, "w_in": w_in_bf}, layer=l, emit_v=False, convert=False,
            side_weights=(w_down[l],))
        x1_s, *w_out_bf = outproj(rows_s, ya_s, yb_s, w_out[l], w_out[l], 1, convert=True,
                                  tm=rows_s.shape[0], tn=OUTPROJ_CONVERT_TN)
        x2_s, w_up_bf = mlp_convert_up(x1_s, norm2[l], w_up[l], w_down_bf, norm_f, final_norm=final_norm,
                                       tf=MLP_TF)
        x1_p, = outproj(rows_p, ya_p, yb_p, *w_out_bf, 0, convert=False,
                        tm=min(rows_p.shape[0], OUTPROJ_TILE[0]), tn=OUTPROJ_TILE[1])
        x2_p = mlp_stream(x1_p, norm2[l], w_up_bf, w_down_bf, norm_f, final_norm=final_norm,
                          tm=min(rows_p.shape[0], ROW_BLOCK), tf=MLP_TF)
        xs, xp = x2_s.reshape(xs.shape), x2_p.reshape(xp.shape)
        s_prompt.append(sp)
        s_sample.append(ss)
        v_sample.append(vs)
    return (xp, xs, jnp.stack(s_prompt), jnp.stack(s_sample), jnp.stack(v_sample))
```

```python
import functools

import jax
import jax.numpy as jnp
from jax import lax
from jax.experimental import pallas as pl
from jax.experimental.pallas import tpu as pltpu

EPS = 1e-6
CHUNK = 64
GMLP_CHUNK = 128
HEADS_A = 8
N_GROUPS = 6
DK_B = 128
MLP_ACC_COLS = 1024
NORM_ROWS = 16
F32_SUBLANES = 8
MLP_WARM_ROWS = 128
BF16 = jnp.bfloat16
F32 = jnp.float32

V7X_VMEM_BYTES = 64 * 1024 * 1024
VMEM_LIMIT_BYTES = V7X_VMEM_BYTES - 4 * 1024 * 1024

ROW_BLOCK = 1024
INPROJ_CONVERT_TK = 1024
OUTPROJ_TILE = (1024, 1024)
OUTPROJ_CONVERT_TN = 512
MLP_TF = 512
HGRN_LONG_BLOCK = (4, 512)


def _params(*semantics):
    return pltpu.CompilerParams(dimension_semantics=semantics, vmem_limit_bytes=VMEM_LIMIT_BYTES)


def _rms_scale(x):
    return lax.rsqrt(jnp.mean(x * x, axis=-1, keepdims=True) + EPS)


def _for_row_chunks(n_rows, body, chunk=NORM_ROWS):
    for r in range(0, n_rows, chunk):
        body(slice(r, r + chunk))


def _inproj_epilogue(pre, nv_ref, lb_ref, ws_ref, bs_ref, out_refs, *, span, layer, emit_v):
    if emit_v:
        ya_ref, v_ref, q_ref, f_ref, i_ref, g_ref = out_refs
    else:
        ya_ref, q_ref, f_ref, i_ref, g_ref = out_refs

    u = jax.nn.gelu(pre(0))
    a = jax.nn.gelu(pre(1))
    v = a * _rms_scale(a) * nv_ref[...]
    if emit_v:
        v_ref[...] = v
    row = lax.broadcasted_iota(jnp.int32, (span, span), 0)
    col = lax.broadcasted_iota(jnp.int32, (span, span), 1)
    w = jnp.where(col // CHUNK <= row // CHUNK, ws_ref[:span, :span], 0.0).astype(BF16)
    bias = bs_ref[:span, :]
    for c in range(u.shape[0] // span):
        sl = slice(c * span, (c + 1) * span)
        mixed = jnp.dot(w, v[sl, :].astype(BF16), preferred_element_type=F32) + bias
        ya_ref[sl, :] = (u[sl, :] * mixed).astype(ya_ref.dtype)

    qz = pre(2)
    q_ref[...] = (qz * jax.nn.sigmoid(qz)).astype(q_ref.dtype)

    lg = lb_ref[...]
    e = jnp.exp(lg - jnp.max(lg, axis=0, keepdims=True))
    sm = e / jnp.sum(e, axis=0, keepdims=True)
    lb = jnp.sum(sm[:layer + 1, :], axis=0, keepdims=True)
    f_ref[...] = lb + (1.0 - lb) * jax.nn.sigmoid(pre(3))

    i_ref[...] = pre(4).astype(i_ref.dtype)

    gz = pre(5)
    g_ref[...] = (gz * jax.nn.sigmoid(gz)).astype(g_ref.dtype)


def _warm_up_rows(x_hbm, warm_ref, sem, n_rows, store_normed):
    warm_rows = warm_ref.shape[0]
    for c in range(n_rows // warm_rows):
        load = pltpu.make_async_copy(x_hbm.at[pl.ds(c * warm_rows, warm_rows), :], warm_ref, sem)
        load.start()
        load.wait()
        for r in range(0, warm_rows, NORM_ROWS):
            store_normed(slice(c * warm_rows + r, c * warm_rows + r + NORM_ROWS), warm_ref[r:r + NORM_ROWS, :])


def _inproj_kernel(x_hbm, n1_ref, *refs, span, layer, emit_v):
    w_refs = refs[:N_GROUPS]
    nv_ref, lb_ref, ws_ref, bs_ref = refs[N_GROUPS:N_GROUPS + 4]
    out_refs = refs[N_GROUPS + 4:-3]
    h_ref, x_buf, sem = refs[-3:]
    i, j = pl.program_id(0), pl.program_id(1)
    tm = h_ref.shape[0]

    def x_rows_copy(block):
        rows = pl.ds(pl.multiple_of(block * tm, tm), tm)
        return pltpu.make_async_copy(x_hbm.at[rows, :], x_buf, sem)

    @pl.when(j == 0)
    def _():
        @pl.when(i == 0)
        def _():
            x_rows_copy(0).start()
        x_rows_copy(i).wait()

        def norm_rows(rs):
            x = x_buf[rs, :]
            h_ref[rs, :] = (x * _rms_scale(x) * n1_ref[...]).astype(h_ref.dtype)
        _for_row_chunks(tm, norm_rows)

    @pl.when(jnp.logical_and(j == 1, i + 1 < pl.num_programs(0)))
    def _():
        x_rows_copy(i + 1).start()

    pre = lambda g: jnp.dot(h_ref[...], w_refs[g][...], preferred_element_type=F32)
    _inproj_epilogue(pre, nv_ref, lb_ref, ws_ref, bs_ref, out_refs, span=span, layer=layer, emit_v=emit_v)


def _inproj_convert_kernel(x_ref, n1_ref, *refs, span, layer, emit_v):
    w_refs = refs[:N_GROUPS]
    nv_ref, lb_ref, ws_ref, bs_ref = refs[N_GROUPS:N_GROUPS + 4]
    n_act = 6 if emit_v else 5
    out_refs = refs[N_GROUPS + 4:N_GROUPS + 4 + n_act]
    wb_refs = refs[N_GROUPS + 4 + n_act:-2]
    h_ref, acc_ref = refs[-2:]
    j, k = pl.program_id(1), pl.program_id(2)
    kt, _, tk = h_ref.shape

    @pl.when(jnp.logical_and(j == 0, k == 0))
    def _():
        def norm_rows(rs):
            x = x_ref[rs, :]
            h = (x * _rms_scale(x) * n1_ref[...]).astype(h_ref.dtype)
            for kk in range(kt):
                h_ref[kk, rs, :] = h[:, kk * tk:(kk + 1) * tk]
        _for_row_chunks(x_ref.shape[0], norm_rows)

    @pl.when(k == 0)
    def _():
        acc_ref[...] = jnp.zeros_like(acc_ref)

    for g in range(N_GROUPS):
        wb = w_refs[g][...].astype(BF16)
        wb_refs[g][...] = wb
        acc_ref[g] += jnp.dot(h_ref[k], wb, preferred_element_type=F32)

    @pl.when(k == kt - 1)
    def _():
        _inproj_epilogue(lambda g: acc_ref[g], nv_ref, lb_ref, ws_ref, bs_ref, out_refs,
                         span=span, layer=layer, emit_v=emit_v)


def inproj(x, norm1, w_groups, norm_v, lb_logits, w_s, b_s, *, span, layer, emit_v, tm):
    m, d = x.shape
    width = w_groups[0].shape[1]
    tn = width // HEADS_A
    tile = pl.BlockSpec((tm, tn), lambda i, j: (i, j))
    out_dtypes = [BF16] + ([F32] if emit_v else []) + [BF16, F32, BF16, BF16]
    kern = functools.partial(_inproj_kernel, span=span, layer=layer, emit_v=emit_v)
    return pl.pallas_call(
        kern,
        grid=(m // tm, width // tn),
        in_specs=[pl.BlockSpec(memory_space=pl.ANY),
                  pl.BlockSpec((1, d), lambda i, j: (0, 0))]
                 + [pl.BlockSpec((d, tn), lambda i, j: (0, j))] * N_GROUPS
                 + [pl.BlockSpec((1, tn), lambda i, j: (0, j)),
                    pl.BlockSpec((lb_logits.shape[0], tn), lambda i, j: (0, j)),
                    pl.BlockSpec((None, GMLP_CHUNK, GMLP_CHUNK), lambda i, j: (j, 0, 0)),
                    pl.BlockSpec((None, GMLP_CHUNK, 1), lambda i, j: (j, 0, 0))],
        out_specs=[tile] * len(out_dtypes),
        out_shape=[jax.ShapeDtypeStruct((m, width), dt) for dt in out_dtypes],
        scratch_shapes=[pltpu.VMEM((tm, d), BF16), pltpu.VMEM((tm, d), F32), pltpu.SemaphoreType.DMA(())],
        compiler_params=_params("arbitrary", "arbitrary"),
        name="inproj",
    )(x, norm1.reshape(1, d), *w_groups, norm_v.reshape(1, width), lb_logits,
      w_s, b_s.reshape(HEADS_A, GMLP_CHUNK, 1))


def inproj_convert(x, norm1, w_in, norm_v, lb_logits, w_s, b_s, *, span, layer, emit_v, tk):
    m, d = x.shape
    width = w_in.shape[1] // N_GROUPS
    tn = width // HEADS_A
    nt, kt = width // tn, d // tk
    w_spec = lambda g: pl.BlockSpec((tk, tn), lambda i, j, k: (k, g * nt + j))
    tile = pl.BlockSpec((m, tn), lambda i, j, k: (i, j))
    out_dtypes = [BF16] + ([F32] if emit_v else []) + [BF16, F32, BF16, BF16]
    kern = functools.partial(_inproj_convert_kernel, span=span, layer=layer, emit_v=emit_v)
    outs = pl.pallas_call(
        kern,
        grid=(1, nt, kt),
        in_specs=[pl.BlockSpec((m, d), lambda i, j, k: (i, 0)),
                  pl.BlockSpec((1, d), lambda i, j, k: (0, 0))]
                 + [w_spec(g) for g in range(N_GROUPS)]
                 + [pl.BlockSpec((1, tn), lambda i, j, k: (0, j)),
                    pl.BlockSpec((lb_logits.shape[0], tn), lambda i, j, k: (0, j)),
                    pl.BlockSpec((None, GMLP_CHUNK, GMLP_CHUNK), lambda i, j, k: (j, 0, 0)),
                    pl.BlockSpec((None, GMLP_CHUNK, 1), lambda i, j, k: (j, 0, 0))],
        out_specs=[tile] * len(out_dtypes)
                  + [pl.BlockSpec((tk, tn), lambda i, j, k: (k, j))] * N_GROUPS,
        out_shape=[jax.ShapeDtypeStruct((m, width), dt) for dt in out_dtypes]
                  + [jax.ShapeDtypeStruct((d, width), BF16)] * N_GROUPS,
        scratch_shapes=[pltpu.VMEM((kt, m, tk), BF16), pltpu.VMEM((N_GROUPS, m, tn), F32)],
        compiler_params=_params("arbitrary", "arbitrary", "arbitrary"),
        name="inproj_convert",
    )(x, norm1.reshape(1, d), *([w_in] * N_GROUPS), norm_v.reshape(1, width), lb_logits,
      w_s, b_s.reshape(HEADS_A, GMLP_CHUNK, 1))
    return outs[:len(out_dtypes)], outs[len(out_dtypes):]


def _split2_bf16(x):
    hi = x.astype(BF16)
    lo = (x - hi.astype(F32)).astype(BF16)
    return hi, lo


def _hgrn_kernel(q_ref, f_ref, i_ref, g_ref, no_ref, s0_ref, *refs, blk, has_s0):
    n_side = (len(refs) - 3) // 2
    side_in, side_out = refs[:n_side], refs[n_side + 2:-1]
    y_ref, s_ref, st_ref = refs[n_side], refs[n_side + 1], refs[-1]
    tb = pl.program_id(2)
    n_streams, tokens, lanes = q_ref.shape
    n_heads, n_chunks = lanes // DK_B, tokens // blk
    chains = [(s, h) for s in range(n_streams) for h in range(n_heads)]
    items = [(s, h, c) for c in range(n_chunks) for s, h in chains]
    tile = lambda ref, s, h, c: ref[s, c * blk:(c + 1) * blk, h * DK_B:(h + 1) * DK_B]

    @pl.when(tb == 0)
    def _():
        for s, h in chains:
            st_ref[s, h] = s0_ref[s, h].T if has_s0 else jnp.zeros((DK_B, DK_B), F32)

    row = lax.broadcasted_iota(jnp.int32, (blk, blk), 0)
    col = lax.broadcasted_iota(jnp.int32, (blk, blk), 1)
    causal = col <= row
    tri = causal.astype(BF16)
    nt = (((1,), (1,)), ((), ()))
    tn = (((0,), (0,)), ((), ()))

    cum = {}
    for it in items:
        hi, lo = _split2_bf16(jnp.log(tile(f_ref, *it)))
        cum[it] = (jnp.dot(tri, hi, preferred_element_type=F32)
                   + jnp.dot(tri, lo, preferred_element_type=F32))

    q_rel, k_rel, q_abs, k_end, decay, val = {}, {}, {}, {}, {}, {}
    for it in items:
        b = cum[it]
        b_mid = b[blk // 2:blk // 2 + 1, :]
        b_end = b[blk - 1:blk, :]
        qr = tile(q_ref, *it) * jnp.exp(b - b_mid)
        kr = (1.0 - tile(f_ref, *it)) * jnp.exp(b_mid - b)
        q_rel[it] = qr.astype(BF16)
        k_rel[it] = kr.astype(BF16)
        q_abs[it] = (qr * jnp.exp(b_mid)).astype(BF16)
        k_end[it] = (kr * jnp.exp(b_end - b_mid)).astype(BF16)
        decay[it] = jnp.exp(b_end)
        val[it] = tile(i_ref, *it).astype(BF16)

    scores = {}
    for it in items:
        sc = lax.dot_general(q_rel[it], k_rel[it], nt, preferred_element_type=F32)
        scores[it] = jnp.where(causal, sc, 0.0).astype(BF16)

    out, delta = {}, {}
    for it in items:
        out[it] = jnp.dot(scores[it], val[it], preferred_element_type=F32)
        delta[it] = lax.dot_general(val[it], k_end[it], tn, preferred_element_type=F32)

    state = {ch: st_ref[ch] for ch in chains}
    for s, h, c in items:
        st = state[s, h]
        out[s, h, c] = out[s, h, c] + lax.dot_general(q_abs[s, h, c], st.astype(BF16), nt,
                                                      preferred_element_type=F32)
        state[s, h] = st * decay[s, h, c] + delta[s, h, c]
    for ch in chains:
        st_ref[ch] = state[ch]

    for s, h, c in items:
        o = out[s, h, c]
        y = o * _rms_scale(o) * no_ref[h] * tile(g_ref, s, h, c)
        y_ref[s, c * blk:(c + 1) * blk, h * DK_B:(h + 1) * DK_B] = y.astype(y_ref.dtype)

    for w_ref, wb_ref in zip(side_in, side_out):
        wb_ref[...] = w_ref[...].astype(wb_ref.dtype)

    @pl.when(tb == pl.num_programs(2) - 1)
    def _():
        for ch in chains:
            s_ref[ch] = st_ref[ch].T


def hgrn(q, f, i, g, norm_o, s0, *, blk, sb, hb, tb, side_weights=()):
    ns, t, width = q.shape
    heads = width // DK_B
    has_s0 = s0 is not None
    grid = (ns // sb, heads // hb, t // tb)
    n_steps = grid[0] * grid[1] * grid[2]
    tok_spec = pl.BlockSpec((sb, tb, hb * DK_B), lambda b, h, c: (b, c, h))
    state_spec = pl.BlockSpec((sb, hb, DK_B, DK_B), lambda b, h, c: (b, h, 0, 0))
    side_specs = []
    for w in side_weights:
        assert w.shape[0] % (n_steps * NORM_ROWS) == 0, "side windows must be whole packed bf16 row tiles"
        side_specs.append(pl.BlockSpec((w.shape[0] // n_steps, w.shape[1]),
                                       lambda b, h, c: ((b * grid[1] + h) * grid[2] + c, 0)))
    if not has_s0:
        s0 = jnp.zeros((ns, heads, DK_B, DK_B), F32)
    kern = functools.partial(_hgrn_kernel, blk=blk, has_s0=has_s0)
    return pl.pallas_call(
        kern,
        grid=grid,
        in_specs=[tok_spec, tok_spec, tok_spec, tok_spec,
                  pl.BlockSpec((hb, 1, DK_B), lambda b, h, c: (h, 0, 0)),
                  state_spec] + side_specs,
        out_specs=[tok_spec, state_spec] + side_specs,
        out_shape=[jax.ShapeDtypeStruct((ns, t, width), BF16),
                   jax.ShapeDtypeStruct((ns, heads, DK_B, DK_B), F32)]
                  + [jax.ShapeDtypeStruct(w.shape, BF16) for w in side_weights],
        scratch_shapes=[pltpu.VMEM((sb, hb, DK_B, DK_B), F32)],
        compiler_params=_params("parallel", "parallel", "arbitrary"),
        name="hgrn",
    )(q, f, i, g, norm_o.reshape(heads, 1, DK_B), s0, *side_weights)


def _outproj_kernel(x_ref, ya_ref, yb_ref, wa_ref, wb_ref, o_ref, *wcopy_refs):
    wa, wb = wa_ref[...], wb_ref[...]
    if wcopy_refs:
        wa, wb = wa.astype(BF16), wb.astype(BF16)
        wcopy_refs[0][...] = wa
        wcopy_refs[1][...] = wb
    o_ref[...] = (x_ref[...]
                  + jnp.dot(ya_ref[...], wa, preferred_element_type=F32)
                  + jnp.dot(yb_ref[...], wb, preferred_element_type=F32))


def outproj(x, ya, yb, w_a, w_b, b_block, *, convert, tm, tn):
    m, d = x.shape
    ka = ya.shape[1]
    w_tile = lambda blk: pl.BlockSpec((ka, tn), lambda i, j: (blk, j))
    out_specs = [pl.BlockSpec((tm, tn), lambda i, j: (i, j))]
    out_shape = [jax.ShapeDtypeStruct((m, d), F32)]
    if convert:
        assert m == tm, "weight copies are written once per row block"
        out_specs += [w_tile(0)] * 2
        out_shape += [jax.ShapeDtypeStruct((ka, d), BF16)] * 2
    return pl.pallas_call(
        _outproj_kernel,
        grid=(m // tm, d // tn),
        in_specs=[pl.BlockSpec((tm, tn), lambda i, j: (i, j)),
                  pl.BlockSpec((tm, ka), lambda i, j: (i, 0)),
                  pl.BlockSpec((tm, ka), lambda i, j: (i, 0)),
                  w_tile(0), w_tile(b_block)],
        out_specs=out_specs,
        out_shape=out_shape,
        compiler_params=_params("parallel", "arbitrary"),
        name="outproj_convert" if convert else "outproj",
    )(x, ya, yb, w_a, w_b)


def _mlp_kernel(x_hbm, n2_ref, wu_ref, wd_ref, nf_ref, o_ref, wub_ref, h_ref, sem, *, final_norm):
    i, f = pl.program_id(0), pl.program_id(1)
    tm = o_ref.shape[0]

    @pl.when(f == 0)
    def _():
        load = pltpu.make_async_copy(x_hbm.at[pl.ds(pl.multiple_of(i * tm, tm), tm), :], o_ref, sem)
        load.start()
        load.wait()

        def norm_rows(rs):
            x = o_ref[rs, :]
            h_ref[rs, :] = (x * _rms_scale(x) * n2_ref[...]).astype(h_ref.dtype)
        _for_row_chunks(tm, norm_rows)

    wu = wu_ref[...].astype(BF16)
    wub_ref[...] = wu
    hid = jnp.dot(h_ref[...], wu, preferred_element_type=F32)
    hid = jnp.square(jnp.maximum(hid, 0.0)).astype(BF16)
    for n in range(o_ref.shape[1] // MLP_ACC_COLS):
        sl = slice(n * MLP_ACC_COLS, (n + 1) * MLP_ACC_COLS)
        o_ref[:, sl] += jnp.dot(hid, wd_ref[:, sl], preferred_element_type=F32)

    if final_norm:
        @pl.when(f == pl.num_programs(1) - 1)
        def _():
            def norm_rows(rs):
                x = o_ref[rs, :]
                o_ref[rs, :] = x * _rms_scale(x) * nf_ref[...]
            _for_row_chunks(o_ref.shape[0], norm_rows)


def mlp_convert_up(x, norm2, w_up, w_down, norm_f, *, final_norm, tf):
    m, d = x.shape
    ff = w_up.shape[1]
    once = pl.Buffered(1)
    up_tile = pl.BlockSpec((d, tf), lambda i, f: (0, f))
    kern = functools.partial(_mlp_kernel, final_norm=final_norm)
    return pl.pallas_call(
        kern,
        grid=(1, ff // tf),
        in_specs=[pl.BlockSpec(memory_space=pl.ANY),
                  pl.BlockSpec((1, d), lambda i, f: (0, 0), pipeline_mode=once),
                  up_tile,
                  pl.BlockSpec((tf, d), lambda i, f: (f, 0)),
                  pl.BlockSpec((1, d), lambda i, f: (0, 0), pipeline_mode=once)],
        out_specs=[pl.BlockSpec((m, d), lambda i, f: (i, 0)), up_tile],
        out_shape=[jax.ShapeDtypeStruct((m, d), F32), jax.ShapeDtypeStruct((d, ff), BF16)],
        scratch_shapes=[pltpu.VMEM((m, d), BF16), pltpu.SemaphoreType.DMA(())],
        compiler_params=_params("arbitrary", "arbitrary"),
        name="mlp_convert_up",
    )(x, norm2.reshape(1, d), w_up, w_down, norm_f.reshape(1, d))


def _mlp_stream_kernel(x_hbm, xn_ref, xc_ref, n2_ref, wu_ref, wd_ref, nf_ref, o_ref, h_ref, warm_ref, sem,
                       *, final_norm):
    i, f = pl.program_id(0), pl.program_id(1)
    tm, d = o_ref.shape
    step_rows = xn_ref.shape[0]
    slot = i % 2
    norm = lambda x: (x * _rms_scale(x) * n2_ref[...]).astype(h_ref.dtype)

    @pl.when(jnp.logical_and(i == 0, f == 0))
    def _():
        def store_normed(rs, x):
            h_ref[0, rs, :] = norm(x)
        _warm_up_rows(x_hbm, warm_ref, sem, tm, store_normed)

    def step(first):
        rs = pl.ds(pl.multiple_of(f * step_rows, step_rows), step_rows)
        if not first:
            o_ref[rs, :] += xc_ref[...]
        hid = jnp.dot(h_ref[slot], wu_ref[...], preferred_element_type=F32)
        hid = jnp.square(jnp.maximum(hid, 0.0)).astype(BF16)
        for n in range(d // MLP_ACC_COLS):
            sl = slice(n * MLP_ACC_COLS, (n + 1) * MLP_ACC_COLS)
            part = jnp.dot(hid, wd_ref[:, sl], preferred_element_type=F32)
            if first:
                o_ref[:, sl] = part
            else:
                o_ref[:, sl] += part
        if first:
            o_ref[rs, :] += xc_ref[...]
        h_ref[1 - slot, rs, :] = norm(xn_ref[...])

    pl.when(f == 0)(lambda: step(True))
    pl.when(f > 0)(lambda: step(False))

    if final_norm:
        @pl.when(f == pl.num_programs(1) - 1)
        def _():
            def norm_rows(rs):
                x = o_ref[rs, :]
                o_ref[rs, :] = x * _rms_scale(x) * nf_ref[...]
            _for_row_chunks(tm, norm_rows, chunk=F32_SUBLANES)


def mlp_stream(x, norm2, w_up, w_down, norm_f, *, final_norm, tm, tf):
    m, d = x.shape
    ff = w_up.shape[1]
    n_row_blocks, n_f = m // tm, ff // tf
    step_rows = tm // n_f
    once = pl.Buffered(1)
    kern = functools.partial(_mlp_stream_kernel, final_norm=final_norm)
    return pl.pallas_call(
        kern,
        grid=(n_row_blocks, n_f),
        in_specs=[pl.BlockSpec(memory_space=pl.ANY),
                  pl.BlockSpec((step_rows, d), lambda i, f: (jnp.minimum(i + 1, n_row_blocks - 1) * n_f + f, 0)),
                  pl.BlockSpec((step_rows, d), lambda i, f: (i * n_f + f, 0)),
                  pl.BlockSpec((1, d), lambda i, f: (0, 0), pipeline_mode=once),
                  pl.BlockSpec((d, tf), lambda i, f: (0, f)),
                  pl.BlockSpec((tf, d), lambda i, f: (f, 0)),
                  pl.BlockSpec((1, d), lambda i, f: (0, 0), pipeline_mode=once)],
        out_specs=pl.BlockSpec((tm, d), lambda i, f: (i, 0), pipeline_mode=once),
        out_shape=jax.ShapeDtypeStruct((m, d), F32),
        scratch_shapes=[pltpu.VMEM((2, tm, d), BF16), pltpu.VMEM((MLP_WARM_ROWS, d), F32),
                        pltpu.SemaphoreType.DMA(())],
        compiler_params=_params("arbitrary", "arbitrary"),
        name="mlp",
    )(x, x, x, norm2.reshape(1, d), w_up, w_down, norm_f.reshape(1, d))


def _mix_heads(x3, s0, lb_logits, p, *, layer, emit_v, convert, side_weights=()):
    ns, t, d = x3.shape
    m = ns * t
    x = x3.reshape(m, d)
    span = min(t, GMLP_CHUNK)
    small = dict(norm_v=p["norm_v"], lb_logits=lb_logits, w_s=p["w_s"], b_s=p["b_s"])
    w_in_bf = None
    if convert:
        outs, w_in_bf = inproj_convert(x, p["norm1"], p["w_in"], **small, span=span, layer=layer,
                                       emit_v=emit_v, tk=INPROJ_CONVERT_TK)
    else:
        outs = inproj(x, p["norm1"], p["w_in"], **small, span=span, layer=layer, emit_v=emit_v,
                      tm=min(m, ROW_BLOCK))
    ya, (q, f, i, g) = outs[0], outs[-4:]
    width = ya.shape[1]
    if t >= HGRN_LONG_BLOCK[1]:
        sb, (hb, tb) = 1, HGRN_LONG_BLOCK
    else:
        sb, hb, tb = ns, 1, t
    per_stream = lambda a: a.reshape(ns, t, width)
    yb3, s_new, *side_bf = hgrn(per_stream(q), per_stream(f), per_stream(i), per_stream(g), p["norm_o"], s0,
                                blk=min(t, CHUNK), sb=sb, hb=hb, tb=tb, side_weights=side_weights)
    v_act = per_stream(outs[1]) if emit_v else None
    return x, ya, yb3.reshape(m, width), s_new, v_act, w_in_bf, side_bf


def kernel(x_prompt, x_sample, state_hgrn, norm1, w_in, w_s, b_s, norm_v, lb_logits, norm_o,
           w_out, norm2, w_up, w_down, norm_f):
    depth = w_in.shape[0]
    lb_logits = lb_logits.astype(F32)
    xp, xs = x_prompt, x_sample
    s_prompt, s_sample, v_sample = [], [], []
    for l in range(depth):
        p = dict(norm1=norm1[l], w_in=w_in[l], w_s=w_s[l], b_s=b_s[l], norm_v=norm_v[l], norm_o=norm_o[l])
        final_norm = l == depth - 1
        rows_s, ya_s, yb_s, ss, vs, w_in_bf, _ = _mix_heads(xs, state_hgrn[l], lb_logits, p, layer=l,
                                                             emit_v=True, convert=True)
        rows_p, ya_p, yb_p, sp, _, _, (w_down_bf,) = _mix_heads(
            xp, None, lb_logits, {**p, "w_in": w_in_bf}, layer=l, emit_v=False, convert=False,
            side_weights=(w_down[l],))
        x1_s, *w_out_bf = outproj(rows_s, ya_s, yb_s, w_out[l], w_out[l], 1, convert=True,
                                  tm=rows_s.shape[0], tn=OUTPROJ_CONVERT_TN)
        x2_s, w_up_bf = mlp_convert_up(x1_s, norm2[l], w_up[l], w_down_bf, norm_f, final_norm=final_norm,
                                       tf=MLP_TF)
        x1_p, = outproj(rows_p, ya_p, yb_p, *w_out_bf, 0, convert=False,
                        tm=min(rows_p.shape[0], OUTPROJ_TILE[0]), tn=OUTPROJ_TILE[1])
        x2_p = mlp_stream(x1_p, norm2[l], w_up_bf, w_down_bf, norm_f, final_norm=final_norm,
                          tm=min(rows_p.shape[0], ROW_BLOCK), tf=MLP_TF)
        xs, xp = x2_s.reshape(xs.shape), x2_p.reshape(xp.shape)
        s_prompt.append(sp)
        s_sample.append(ss)
        v_sample.append(vs)
    return (xp, xs, jnp.stack(s_prompt), jnp.stack(s_sample), jnp.stack(v_sample))
```

```python
import functools

import jax
import jax.numpy as jnp
from jax import lax
from jax.experimental import pallas as pl
from jax.experimental.pallas import tpu as pltpu

EPS = 1e-6
CHUNK = 64
GMLP_CHUNK = 128
HEADS_A = 8
N_GROUPS = 6
DK_B = 128
MLP_ACC_COLS = 1024
NORM_ROWS = 16
F32_SUBLANES = 8
MLP_WARM_ROWS = 128
BF16 = jnp.bfloat16
F32 = jnp.float32

V7X_VMEM_BYTES = 64 * 1024 * 1024
VMEM_LIMIT_BYTES = V7X_VMEM_BYTES - 4 * 1024 * 1024

ROW_BLOCK = 1024
INPROJ_CONVERT_TK = 1024
OUTPROJ_TILE = (1024, 1024)
OUTPROJ_CONVERT_TN = 512
MLP_TF = 512
HGRN_LONG_BLOCK = (4, 512)


def _params(*semantics):
    return pltpu.CompilerParams(dimension_semantics=semantics, vmem_limit_bytes=VMEM_LIMIT_BYTES)


def _rms_scale(x):
    return lax.rsqrt(jnp.mean(x * x, axis=-1, keepdims=True) + EPS)


def _for_row_chunks(n_rows, body, chunk=NORM_ROWS):
    for r in range(0, n_rows, chunk):
        body(slice(r, r + chunk))


def _inproj_epilogue(pre, nv_ref, lb_ref, ws_ref, bs_ref, out_refs, *, span, layer, emit_v):
    if emit_v:
        ya_ref, v_ref, q_ref, f_ref, i_ref, g_ref = out_refs
    else:
        ya_ref, q_ref, f_ref, i_ref, g_ref = out_refs

    u = jax.nn.gelu(pre(0))
    a = jax.nn.gelu(pre(1))
    v = a * _rms_scale(a) * nv_ref[...]
    if emit_v:
        v_ref[...] = v
    row = lax.broadcasted_iota(jnp.int32, (span, span), 0)
    col = lax.broadcasted_iota(jnp.int32, (span, span), 1)
    w = jnp.where(col // CHUNK <= row // CHUNK, ws_ref[:span, :span], 0.0).astype(BF16)
    bias = bs_ref[:span, :]
    for c in range(u.shape[0] // span):
        sl = slice(c * span, (c + 1) * span)
        mixed = jnp.dot(w, v[sl, :].astype(BF16), preferred_element_type=F32) + bias
        ya_ref[sl, :] = (u[sl, :] * mixed).astype(ya_ref.dtype)

    qz = pre(2)
    q_ref[...] = (qz * jax.nn.sigmoid(qz)).astype(q_ref.dtype)

    lg = lb_ref[...]
    e = jnp.exp(lg - jnp.max(lg, axis=0, keepdims=True))
    sm = e / jnp.sum(e, axis=0, keepdims=True)
    lb = jnp.sum(sm[:layer + 1, :], axis=0, keepdims=True)
    f_ref[...] = lb + (1.0 - lb) * jax.nn.sigmoid(pre(3))

    i_ref[...] = pre(4).astype(i_ref.dtype)

    gz = pre(5)
    g_ref[...] = (gz * jax.nn.sigmoid(gz)).astype(g_ref.dtype)


def _warm_up_rows(x_hbm, warm_ref, sem, n_rows, store_normed):
    warm_rows = warm_ref.shape[0]
    for c in range(n_rows // warm_rows):
        load = pltpu.make_async_copy(x_hbm.at[pl.ds(c * warm_rows, warm_rows), :], warm_ref, sem)
        load.start()
        load.wait()
        for r in range(0, warm_rows, NORM_ROWS):
            store_normed(slice(c * warm_rows + r, c * warm_rows + r + NORM_ROWS), warm_ref[r:r + NORM_ROWS, :])


def _inproj_kernel(x_hbm, n1_ref, *refs, span, layer, emit_v):
    w_refs = refs[:N_GROUPS]
    nv_ref, lb_ref, ws_ref, bs_ref = refs[N_GROUPS:N_GROUPS + 4]
    out_refs = refs[N_GROUPS + 4:-3]
    h_ref, x_buf, sem = refs[-3:]
    i, j = pl.program_id(0), pl.program_id(1)
    tm = h_ref.shape[0]

    def x_rows_copy(block):
        rows = pl.ds(pl.multiple_of(block * tm, tm), tm)
        return pltpu.make_async_copy(x_hbm.at[rows, :], x_buf, sem)

    @pl.when(j == 0)
    def _():
        @pl.when(i == 0)
        def _():
            x_rows_copy(0).start()
        x_rows_copy(i).wait()

        def norm_rows(rs):
            x = x_buf[rs, :]
            h_ref[rs, :] = (x * _rms_scale(x) * n1_ref[...]).astype(h_ref.dtype)
        _for_row_chunks(tm, norm_rows)

    @pl.when(jnp.logical_and(j == 1, i + 1 < pl.num_programs(0)))
    def _():
        x_rows_copy(i + 1).start()

    pre = lambda g: jnp.dot(h_ref[...], w_refs[g][...], preferred_element_type=F32)
    _inproj_epilogue(pre, nv_ref, lb_ref, ws_ref, bs_ref, out_refs, span=span, layer=layer, emit_v=emit_v)


def _inproj_convert_kernel(x_ref, n1_ref, *refs, span, layer, emit_v):
    w_refs = refs[:N_GROUPS]
    nv_ref, lb_ref, ws_ref, bs_ref = refs[N_GROUPS:N_GROUPS + 4]
    n_act = 6 if emit_v else 5
    out_refs = refs[N_GROUPS + 4:N_GROUPS + 4 + n_act]
    wb_refs = refs[N_GROUPS + 4 + n_act:-2]
    h_ref, acc_ref = refs[-2:]
    j, k = pl.program_id(1), pl.program_id(2)
    kt, _, tk = h_ref.shape

    @pl.when(jnp.logical_and(j == 0, k == 0))
    def _():
        def norm_rows(rs):
            x = x_ref[rs, :]
            h = (x * _rms_scale(x) * n1_ref[...]).astype(h_ref.dtype)
            for kk in range(kt):
                h_ref[kk, rs, :] = h[:, kk * tk:(kk + 1) * tk]
        _for_row_chunks(x_ref.shape[0], norm_rows)

    @pl.when(k == 0)
    def _():
        acc_ref[...] = jnp.zeros_like(acc_ref)

    for g in range(N_GROUPS):
        wb = w_refs[g][...].astype(BF16)
        wb_refs[g][...] = wb
        acc_ref[g] += jnp.dot(h_ref[k], wb, preferred_element_type=F32)

    @pl.when(k == kt - 1)
    def _():
        _inproj_epilogue(lambda g: acc_ref[g], nv_ref, lb_ref, ws_ref, bs_ref, out_refs,
                         span=span, layer=layer, emit_v=emit_v)


def inproj(x, norm1, w_groups, norm_v, lb_logits, w_s, b_s, *, span, layer, emit_v, tm):
    m, d = x.shape
    width = w_groups[0].shape[1]
    tn = width // HEADS_A
    tile = pl.BlockSpec((tm, tn), lambda i, j: (i, j))
    out_dtypes = [BF16] + ([F32] if emit_v else []) + [BF16, F32, BF16, BF16]
    kern = functools.partial(_inproj_kernel, span=span, layer=layer, emit_v=emit_v)
    return pl.pallas_call(
        kern,
        grid=(m // tm, width // tn),
        in_specs=[pl.BlockSpec(memory_space=pl.ANY),
                  pl.BlockSpec((1, d), lambda i, j: (0, 0))]
                 + [pl.BlockSpec((d, tn), lambda i, j: (0, j))] * N_GROUPS
                 + [pl.BlockSpec((1, tn), lambda i, j: (0, j)),
                    pl.BlockSpec((lb_logits.shape[0], tn), lambda i, j: (0, j)),
                    pl.BlockSpec((None, GMLP_CHUNK, GMLP_CHUNK), lambda i, j: (j, 0, 0)),
                    pl.BlockSpec((None, GMLP_CHUNK, 1), lambda i, j: (j, 0, 0))],
        out_specs=[tile] * len(out_dtypes),
        out_shape=[jax.ShapeDtypeStruct((m, width), dt) for dt in out_dtypes],
        scratch_shapes=[pltpu.VMEM((tm, d), BF16), pltpu.VMEM((tm, d), F32), pltpu.SemaphoreType.DMA(())],
        compiler_params=_params("arbitrary", "arbitrary"),
        name="inproj",
    )(x, norm1.reshape(1, d), *w_groups, norm_v.reshape(1, width), lb_logits,
      w_s, b_s.reshape(HEADS_A, GMLP_CHUNK, 1))


def inproj_convert(x, norm1, w_in, norm_v, lb_logits, w_s, b_s, *, span, layer, emit_v, tk):
    m, d = x.shape
    width = w_in.shape[1] // N_GROUPS
    tn = width // HEADS_A
    nt, kt = width // tn, d // tk
    w_spec = lambda g: pl.BlockSpec((tk, tn), lambda i, j, k: (k, g * nt + j))
    tile = pl.BlockSpec((m, tn), lambda i, j, k: (i, j))
    out_dtypes = [BF16] + ([F32] if emit_v else []) + [BF16, F32, BF16, BF16]
    kern = functools.partial(_inproj_convert_kernel, span=span, layer=layer, emit_v=emit_v)
    outs = pl.pallas_call(
        kern,
        grid=(1, nt, kt),
        in_specs=[pl.BlockSpec((m, d), lambda i, j, k: (i, 0)),
                  pl.BlockSpec((1, d), lambda i, j, k: (0, 0))]
                 + [w_spec(g) for g in range(N_GROUPS)]
                 + [pl.BlockSpec((1, tn), lambda i, j, k: (0, j)),
                    pl.BlockSpec((lb_logits.shape[0], tn), lambda i, j, k: (0, j)),
                    pl.BlockSpec((None, GMLP_CHUNK, GMLP_CHUNK), lambda i, j, k: (j, 0, 0)),
                    pl.BlockSpec((None, GMLP_CHUNK, 1), lambda i, j, k: (j, 0, 0))],
        out_specs=[tile] * len(out_dtypes)
                  + [pl.BlockSpec((tk, tn), lambda i, j, k: (k, j))] * N_GROUPS,
        out_shape=[jax.ShapeDtypeStruct((m, width), dt) for dt in out_dtypes]
                  + [jax.ShapeDtypeStruct((d, width), BF16)] * N_GROUPS,
        scratch_shapes=[pltpu.VMEM((kt, m, tk), BF16), pltpu.VMEM((N_GROUPS, m, tn), F32)],
        compiler_params=_params("arbitrary", "arbitrary", "arbitrary"),
        name="inproj_convert",
    )(x, norm1.reshape(1, d), *([w_in] * N_GROUPS), norm_v.reshape(1, width), lb_logits,
      w_s, b_s.reshape(HEADS_A, GMLP_CHUNK, 1))
    return outs[:len(out_dtypes)], outs[len(out_dtypes):]


def _split2_bf16(x):
    hi = x.astype(BF16)
    lo = (x - hi.astype(F32)).astype(BF16)
    return hi, lo


def _hgrn_kernel(q_ref, f_ref, i_ref, g_ref, no_ref, s0_ref, *refs, blk, has_s0):
    n_side = (len(refs) - 3) // 2
    side_in, side_out = refs[:n_side], refs[n_side + 2:-1]
    y_ref, s_ref, st_ref = refs[n_side], refs[n_side + 1], refs[-1]
    tb = pl.program_id(2)
    n_streams, tokens, lanes = q_ref.shape
    n_heads, n_chunks = lanes // DK_B, tokens // blk
    chains = [(s, h) for s in range(n_streams) for h in range(n_heads)]
    items = [(s, h, c) for c in range(n_chunks) for s, h in chains]
    tile = lambda ref, s, h, c: ref[s, c * blk:(c + 1) * blk, h * DK_B:(h + 1) * DK_B]

    @pl.when(tb == 0)
    def _():
        for s, h in chains:
            st_ref[s, h] = s0_ref[s, h].T if has_s0 else jnp.zeros((DK_B, DK_B), F32)

    row = lax.broadcasted_iota(jnp.int32, (blk, blk), 0)
    col = lax.broadcasted_iota(jnp.int32, (blk, blk), 1)
    causal = col <= row
    tri = causal.astype(BF16)
    nt = (((1,), (1,)), ((), ()))
    tn = (((0,), (0,)), ((), ()))

    cum = {}
    for it in items:
        hi, lo = _split2_bf16(jnp.log(tile(f_ref, *it)))
        cum[it] = (jnp.dot(tri, hi, preferred_element_type=F32)
                   + jnp.dot(tri, lo, preferred_element_type=F32))

    q_rel, k_rel, q_abs, k_end, decay, val = {}, {}, {}, {}, {}, {}
    for it in items:
        b = cum[it]
        b_mid = b[blk // 2:blk // 2 + 1, :]
        b_end = b[blk - 1:blk, :]
        qr = tile(q_ref, *it) * jnp.exp(b - b_mid)
        kr = (1.0 - tile(f_ref, *it)) * jnp.exp(b_mid - b)
        q_rel[it] = qr.astype(BF16)
        k_rel[it] = kr.astype(BF16)
        q_abs[it] = (qr * jnp.exp(b_mid)).astype(BF16)
        k_end[it] = (kr * jnp.exp(b_end - b_mid)).astype(BF16)
        decay[it] = jnp.exp(b_end)
        val[it] = tile(i_ref, *it).astype(BF16)

    scores = {}
    for it in items:
        sc = lax.dot_general(q_rel[it], k_rel[it], nt, preferred_element_type=F32)
        scores[it] = jnp.where(causal, sc, 0.0).astype(BF16)

    out, delta = {}, {}
    for it in items:
        out[it] = jnp.dot(scores[it], val[it], preferred_element_type=F32)
        delta[it] = lax.dot_general(val[it], k_end[it], tn, preferred_element_type=F32)

    state = {ch: st_ref[ch] for ch in chains}
    for s, h, c in items:
        st = state[s, h]
        out[s, h, c] = out[s, h, c] + lax.dot_general(q_abs[s, h, c], st.astype(BF16), nt,
                                                      preferred_element_type=F32)
        state[s, h] = st * decay[s, h, c] + delta[s, h, c]
    for ch in chains:
        st_ref[ch] = state[ch]

    for s, h, c in items:
        o = out[s, h, c]
        y = o * _rms_scale(o) * no_ref[h] * tile(g_ref, s, h, c)
        y_ref[s, c * blk:(c + 1) * blk, h * DK_B:(h + 1) * DK_B] = y.astype(y_ref.dtype)

    for w_ref, wb_ref in zip(side_in, side_out):
        wb_ref[...] = w_ref[...].astype(wb_ref.dtype)

    @pl.when(tb == pl.num_programs(2) - 1)
    def _():
        for ch in chains:
            s_ref[ch] = st_ref[ch].T


def hgrn(q, f, i, g, norm_o, s0, *, blk, sb, hb, tb, side_weights=()):
    ns, t, width = q.shape
    heads = width // DK_B
    has_s0 = s0 is not None
    grid = (ns // sb, heads // hb, t // tb)
    n_steps = grid[0] * grid[1] * grid[2]
    tok_spec = pl.BlockSpec((sb, tb, hb * DK_B), lambda b, h, c: (b, c, h))
    state_spec = pl.BlockSpec((sb, hb, DK_B, DK_B), lambda b, h, c: (b, h, 0, 0))
    side_specs = []
    for w in side_weights:
        assert w.shape[0] % (n_steps * NORM_ROWS) == 0, "side windows must be whole packed bf16 row tiles"
        side_specs.append(pl.BlockSpec((w.shape[0] // n_steps, w.shape[1]),
                                       lambda b, h, c: ((b * grid[1] + h) * grid[2] + c, 0)))
    if not has_s0:
        s0 = jnp.zeros((ns, heads, DK_B, DK_B), F32)
    kern = functools.partial(_hgrn_kernel, blk=blk, has_s0=has_s0)
    return pl.pallas_call(
        kern,
        grid=grid,
        in_specs=[tok_spec, tok_spec, tok_spec, tok_spec,
                  pl.BlockSpec((hb, 1, DK_B), lambda b, h, c: (h, 0, 0)),
                  state_spec] + side_specs,
        out_specs=[tok_spec, state_spec] + side_specs,
        out_shape=[jax.ShapeDtypeStruct((ns, t, width), BF16),
                   jax.ShapeDtypeStruct((ns, heads, DK_B, DK_B), F32)]
                  + [jax.ShapeDtypeStruct(w.shape, BF16) for w in side_weights],
        scratch_shapes=[pltpu.VMEM((sb, hb, DK_B, DK_B), F32)],
        compiler_params=_params("parallel", "parallel", "arbitrary"),
        name="hgrn",
    )(q, f, i, g, norm_o.reshape(heads, 1, DK_B), s0, *side_weights)


def _outproj_kernel(x_ref, ya_ref, yb_ref, wa_ref, wb_ref, o_ref, *wcopy_refs):
    wa, wb = wa_ref[...], wb_ref[...]
    if wcopy_refs:
        wa, wb = wa.astype(BF16), wb.astype(BF16)
        wcopy_refs[0][...] = wa
        wcopy_refs[1][...] = wb
    o_ref[...] = (x_ref[...]
                  + jnp.dot(ya_ref[...], wa, preferred_element_type=F32)
                  + jnp.dot(yb_ref[...], wb, preferred_element_type=F32))


def outproj(x, ya, yb, w_a, w_b, b_block, *, convert, tm, tn):
    m, d = x.shape
    ka = ya.shape[1]
    w_tile = lambda blk: pl.BlockSpec((ka, tn), lambda i, j: (blk, j))
    out_specs = [pl.BlockSpec((tm, tn), lambda i, j: (i, j))]
    out_shape = [jax.ShapeDtypeStruct((m, d), F32)]
    if convert:
        assert m == tm, "weight copies are written once per row block"
        out_specs += [w_tile(0)] * 2
        out_shape += [jax.ShapeDtypeStruct((ka, d), BF16)] * 2
    return pl.pallas_call(
        _outproj_kernel,
        grid=(m // tm, d // tn),
        in_specs=[pl.BlockSpec((tm, tn), lambda i, j: (i, j)),
                  pl.BlockSpec((tm, ka), lambda i, j: (i, 0)),
                  pl.BlockSpec((tm, ka), lambda i, j: (i, 0)),
                  w_tile(0), w_tile(b_block)],
        out_specs=out_specs,
        out_shape=out_shape,
        compiler_params=_params("parallel", "arbitrary"),
        name="outproj_convert" if convert else "outproj",
    )(x, ya, yb, w_a, w_b)


def _mlp_kernel(x_hbm, n2_ref, wu_ref, wd_ref, nf_ref, o_ref, wub_ref, h_ref, sem, *, final_norm):
    i, f = pl.program_id(0), pl.program_id(1)
    tm = o_ref.shape[0]

    @pl.when(f == 0)
    def _():
        load = pltpu.make_async_copy(x_hbm.at[pl.ds(pl.multiple_of(i * tm, tm), tm), :], o_ref, sem)
        load.start()
        load.wait()

        def norm_rows(rs):
            x = o_ref[rs, :]
            h_ref[rs, :] = (x * _rms_scale(x) * n2_ref[...]).astype(h_ref.dtype)
        _for_row_chunks(tm, norm_rows)

    wu = wu_ref[...].astype(BF16)
    wub_ref[...] = wu
    hid = jnp.dot(h_ref[...], wu, preferred_element_type=F32)
    hid = jnp.square(jnp.maximum(hid, 0.0)).astype(BF16)
    for n in range(o_ref.shape[1] // MLP_ACC_COLS):
        sl = slice(n * MLP_ACC_COLS, (n + 1) * MLP_ACC_COLS)
        o_ref[:, sl] += jnp.dot(hid, wd_ref[:, sl], preferred_element_type=F32)

    if final_norm:
        @pl.when(f == pl.num_programs(1) - 1)
        def _():
            def norm_rows(rs):
                x = o_ref[rs, :]
                o_ref[rs, :] = x * _rms_scale(x) * nf_ref[...]
            _for_row_chunks(o_ref.shape[0], norm_rows)


def mlp_convert_up(x, norm2, w_up, w_down, norm_f, *, final_norm, tf):
    m, d = x.shape
    ff = w_up.shape[1]
    once = pl.Buffered(1)
    up_tile = pl.BlockSpec((d, tf), lambda i, f: (0, f))
    kern = functools.partial(_mlp_kernel, final_norm=final_norm)
    return pl.pallas_call(
        kern,
        grid=(1, ff // tf),
        in_specs=[pl.BlockSpec(memory_space=pl.ANY),
                  pl.BlockSpec((1, d), lambda i, f: (0, 0), pipeline_mode=once),
                  up_tile,
                  pl.BlockSpec((tf, d), lambda i, f: (f, 0)),
                  pl.BlockSpec((1, d), lambda i, f: (0, 0), pipeline_mode=once)],
        out_specs=[pl.BlockSpec((m, d), lambda i, f: (i, 0)), up_tile],
        out_shape=[jax.ShapeDtypeStruct((m, d), F32), jax.ShapeDtypeStruct((d, ff), BF16)],
        scratch_shapes=[pltpu.VMEM((m, d), BF16), pltpu.SemaphoreType.DMA(())],
        compiler_params=_params("arbitrary", "arbitrary"),
        name="mlp_convert_up",
    )(x, norm2.reshape(1, d), w_up, w_down, norm_f.reshape(1, d))


def _mlp_pipe_kernel(x_hbm, wu_hbm, wd_hbm, n2_ref, nf_ref, o_ref, h_ref, warm_ref, sem, *, final_norm, tf):
    i = pl.program_id(0)
    tm, d = o_ref.shape
    n_f = wu_hbm.shape[1] // tf
    step_rows = tm // n_f
    slot = i % 2
    nxt = jnp.minimum(i + 1, pl.num_programs(0) - 1)
    norm = lambda x: (x * _rms_scale(x) * n2_ref[...]).astype(h_ref.dtype)

    @pl.when(i == 0)
    def _():
        def store_normed(rs, x):
            h_ref[0, rs, :] = norm(x)
        _warm_up_rows(x_hbm, warm_ref, sem, tm, store_normed)

    def inner(idx, wu_ref, wd_ref, xn_ref, xc_ref):
        f = idx[0]
        rs = pl.ds(pl.multiple_of(f * step_rows, step_rows), step_rows)

        def step(first):
            if not first:
                o_ref[rs, :] += xc_ref[...]
            hid = jnp.dot(h_ref[slot], wu_ref[...], preferred_element_type=F32)
            hid = jnp.square(jnp.maximum(hid, 0.0)).astype(BF16)
            for n in range(d // MLP_ACC_COLS):
                sl = slice(n * MLP_ACC_COLS, (n + 1) * MLP_ACC_COLS)
                part = jnp.dot(hid, wd_ref[:, sl], preferred_element_type=F32)
                if first:
                    o_ref[:, sl] = part
                else:
                    o_ref[:, sl] += part
            if first:
                o_ref[rs, :] += xc_ref[...]
            h_ref[1 - slot, rs, :] = norm(xn_ref[...])

        pl.when(f == 0)(lambda: step(True))
        pl.when(f > 0)(lambda: step(False))

    pltpu.emit_pipeline(
        inner,
        grid=(n_f,),
        in_specs=[pl.BlockSpec((d, tf), lambda f: (0, f)),
                  pl.BlockSpec((tf, d), lambda f: (f, 0)),
                  pl.BlockSpec((step_rows, d), lambda f: (nxt * n_f + f, 0)),
                  pl.BlockSpec((step_rows, d), lambda f: (i * n_f + f, 0))],
        _explicit_indices=True,
    )(wu_hbm, wd_hbm, x_hbm, x_hbm)

    if final_norm:
        def norm_rows(rs):
            x = o_ref[rs, :]
            o_ref[rs, :] = x * _rms_scale(x) * nf_ref[...]
        _for_row_chunks(tm, norm_rows, chunk=F32_SUBLANES)


def mlp_stream(x, norm2, w_up, w_down, norm_f, *, final_norm, tm, tf):
    m, d = x.shape
    once = pl.Buffered(1)
    any_spec = pl.BlockSpec(memory_space=pl.ANY)
    kern = functools.partial(_mlp_pipe_kernel, final_norm=final_norm, tf=tf)
    return pl.pallas_call(
        kern,
        grid=(m // tm,),
        in_specs=[any_spec, any_spec, any_spec,
                  pl.BlockSpec((1, d), lambda i: (0, 0), pipeline_mode=once),
                  pl.BlockSpec((1, d), lambda i: (0, 0), pipeline_mode=once)],
        out_specs=pl.BlockSpec((tm, d), lambda i: (i, 0), pipeline_mode=once),
        out_shape=jax.ShapeDtypeStruct((m, d), F32),
        scratch_shapes=[pltpu.VMEM((2, tm, d), BF16), pltpu.VMEM((MLP_WARM_ROWS, d), F32),
                        pltpu.SemaphoreType.DMA(())],
        compiler_params=_params("arbitrary"),
        name="mlp",
    )(x, w_up, w_down, norm2.reshape(1, d), norm_f.reshape(1, d))


def _mix_heads(x3, s0, lb_logits, p, *, layer, emit_v, convert, side_weights=()):
    ns, t, d = x3.shape
    m = ns * t
    x = x3.reshape(m, d)
    span = min(t, GMLP_CHUNK)
    small = dict(norm_v=p["norm_v"], lb_logits=lb_logits, w_s=p["w_s"], b_s=p["b_s"])
    w_in_bf = None
    if convert:
        outs, w_in_bf = inproj_convert(x, p["norm1"], p["w_in"], **small, span=span, layer=layer,
                                       emit_v=emit_v, tk=INPROJ_CONVERT_TK)
    else:
        outs = inproj(x, p["norm1"], p["w_in"], **small, span=span, layer=layer, emit_v=emit_v,
                      tm=min(m, ROW_BLOCK))
    ya, (q, f, i, g) = outs[0], outs[-4:]
    width = ya.shape[1]
    if t >= HGRN_LONG_BLOCK[1]:
        sb, (hb, tb) = 1, HGRN_LONG_BLOCK
    else:
        sb, hb, tb = ns, 1, t
    per_stream = lambda a: a.reshape(ns, t, width)
    yb3, s_new, *side_bf = hgrn(per_stream(q), per_stream(f), per_stream(i), per_stream(g), p["norm_o"], s0,
                                blk=min(t, CHUNK), sb=sb, hb=hb, tb=tb, side_weights=side_weights)
    v_act = per_stream(outs[1]) if emit_v else None
    return x, ya, yb3.reshape(m, width), s_new, v_act, w_in_bf, side_bf


def kernel(x_prompt, x_sample, state_hgrn, norm1, w_in, w_s, b_s, norm_v, lb_logits, norm_o,
           w_out, norm2, w_up, w_down, norm_f):
    depth = w_in.shape[0]
    lb_logits = lb_logits.astype(F32)
    xp, xs = x_prompt, x_sample
    s_prompt, s_sample, v_sample = [], [], []
    for l in range(depth):
        p = dict(norm1=norm1[l], w_in=w_in[l], w_s=w_s[l], b_s=b_s[l], norm_v=norm_v[l], norm_o=norm_o[l])
        final_norm = l == depth - 1
        rows_s, ya_s, yb_s, ss, vs, w_in_bf, _ = _mix_heads(xs, state_hgrn[l], lb_logits, p, layer=l,
                                                             emit_v=True, convert=True)
        rows_p, ya_p, yb_p, sp, _, _, (w_down_bf,) = _mix_heads(
            xp, None, lb_logits, {**p, "w_in": w_in_bf}, layer=l, emit_v=False, convert=False,
            side_weights=(w_down[l],))
        x1_s, *w_out_bf = outproj(rows_s, ya_s, yb_s, w_out[l], w_out[l], 1, convert=True,
                                  tm=rows_s.shape[0], tn=OUTPROJ_CONVERT_TN)
        x2_s, w_up_bf = mlp_convert_up(x1_s, norm2[l], w_up[l], w_down_bf, norm_f, final_norm=final_norm,
                                       tf=MLP_TF)
        x1_p, = outproj(rows_p, ya_p, yb_p, *w_out_bf, 0, convert=False,
                        tm=min(rows_p.shape[0], OUTPROJ_TILE[0]), tn=OUTPROJ_TILE[1])
        x2_p = mlp_stream(x1_p, norm2[l], w_up_bf, w_down_bf, norm_f, final_norm=final_norm,
                          tm=min(rows_p.shape[0], ROW_BLOCK), tf=MLP_TF)
        xs, xp = x2_s.reshape(xs.shape), x2_p.reshape(xp.shape)
        s_prompt.append(sp)
        s_sample.append(ss)
        v_sample.append(vs)
    return (xp, xs, jnp.stack(s_prompt), jnp.stack(s_sample), jnp.stack(v_sample))
```
